```python
import jax, jax.numpy as jnp
from jax import lax
import numpy as np

D_MODEL = 2048
BATCH = 8
SEQ = 2048
DEPTH = 1
DEC_BATCH = 2
DEC_SEQ = 16384
PAST_LEN = 128

GRID_W = 64
ATTN_WIDTH = D_MODEL // 2
N_HEADS = 8
HEAD_DIM = ATTN_WIDTH // N_HEADS
WIN_R = 8
WIN_C = 16
POOL_WIDTH = D_MODEL - ATTN_WIDTH
POOL_WINDOWS = (2, 4, 8, 16)
N_POOL_GROUPS = len(POOL_WINDOWS)
POOL_CH = POOL_WIDTH // N_POOL_GROUPS
MIX_WIDTH = ATTN_WIDTH + POOL_WIDTH
IN_WIDTH = 3 * ATTN_WIDTH + POOL_WIDTH
N_EXPERTS = 16
CAPACITY_FACTOR = 2
EXPERT_FF = ((8 * D_MODEL // 3 + 255) // 256 * 256) // 2
EPS = 1e-6

kernel_name = "hybrid_natten_pool_ec_encoder"


def rmsnorm(x, gain):
    xf = x.astype(jnp.float32)
    xf = xf * lax.rsqrt(jnp.mean(xf * xf, axis=-1, keepdims=True) + EPS)
    return (xf * gain.astype(jnp.float32)).astype(x.dtype)


def neighbourhood_attention(q, k, v, rpb):
    B, T, H, Dh = q.shape
    rows = T // GRID_W
    wr = min(WIN_R, rows)
    q = q.reshape(B, rows, GRID_W, H, Dh)
    k = k.reshape(B, rows, GRID_W, H, Dh)
    v = v.reshape(B, rows, GRID_W, H, Dh)
    cols = jnp.arange(GRID_W)
    col_start = jnp.clip(cols - WIN_C // 2, 0, GRID_W - WIN_C)
    col_idx = col_start[:, None] + jnp.arange(WIN_C)[None, :]
    dc_idx = col_idx - cols[:, None] + (WIN_C - 1)

    def one_row(r):
        r0 = jnp.clip(r - wr // 2, 0, rows - wr)
        k_rows = lax.dynamic_slice_in_dim(k, r0, wr, axis=1)
        v_rows = lax.dynamic_slice_in_dim(v, r0, wr, axis=1)
        k_win = k_rows[:, :, col_idx]
        v_win = v_rows[:, :, col_idx]
        q_row = lax.dynamic_index_in_dim(q, r, axis=1, keepdims=False)
        s = jnp.einsum('bchd,brcwhd->bhcrw', q_row, k_win,
                       preferred_element_type=jnp.float32)
        dr_idx = r0 + jnp.arange(wr) - r + (WIN_R - 1)
        bias = rpb[:, dr_idx[None, :, None], dc_idx[:, None, :]]
        s = s + bias.astype(jnp.float32)[None]
        p = jax.nn.softmax(s.reshape(B, H, GRID_W, wr * WIN_C), axis=-1).reshape(s.shape)
        return jnp.einsum('bhcrw,brcwhd->bchd', p.astype(v.dtype), v_win)

    out = lax.map(one_row, jnp.arange(rows))
    return out.transpose(1, 0, 2, 3, 4).reshape(B, T, H * Dh)


def multiscale_pool(u, w_pool, pool_scale):
    B, T, C = u.shape
    uf = u.astype(jnp.float32)
    cs = jnp.concatenate([jnp.zeros((B, 1, C), jnp.float32), jnp.cumsum(uf, axis=1)], axis=1)
    t = jnp.arange(T)
    outs = []
    for g, w in enumerate(POOL_WINDOWS):
        sl = slice(g * POOL_CH, (g + 1) * POOL_CH)
        lo = jnp.clip(t - w // 2, 0, T)
        hi = jnp.clip(t + w // 2, 0, T)
        csg = cs[:, :, sl]
        mean = (csg[:, hi] - csg[:, lo]) / (hi - lo).astype(jnp.float32)[None, :, None]
        outs.append(mean - uf[:, :, sl])
    p = jnp.stack(outs, axis=2).astype(u.dtype)
    y = jnp.einsum('btgc,gce->btge', p, w_pool).reshape(B, T, C)
    return y * pool_scale


def expert_choice_ffn(h, w_router, w_gate, w_up, w_down):
    B, T, D = h.shape
    n_tok = B * T
    hf = h.reshape(n_tok, D)
    aff = jax.nn.softmax(jnp.matmul(hf, w_router).astype(jnp.float32), axis=-1)
    cap = max(1, (CAPACITY_FACTOR * n_tok) // N_EXPERTS)
    gate, idx = lax.top_k(aff.T, cap)
    xe = hf[idx]
    a = jnp.einsum('ecd,edf->ecf', xe, w_gate)
    b = jnp.einsum('ecd,edf->ecf', xe, w_up)
    y = jnp.einsum('ecf,efd->ecd', jax.nn.silu(a) * b, w_down)
    y = y * gate[..., None].astype(y.dtype)
    out = jnp.zeros((n_tok, D), y.dtype).at[idx.reshape(-1)].add(y.reshape(-1, D))
    return out.reshape(B, T, D)


def encoder_layer(x, norm_mix, w_in, q_norm, k_norm, rel_pos_bias, w_pool, pool_scale,
                  w_out, norm_ffn, w_router, w_gate, w_up, w_down):
    B, T, _ = x.shape
    h = rmsnorm(x, norm_mix)
    z = jnp.matmul(h, w_in)
    q = z[..., :ATTN_WIDTH].reshape(B, T, N_HEADS, HEAD_DIM)
    k = z[..., ATTN_WIDTH:2 * ATTN_WIDTH].reshape(B, T, N_HEADS, HEAD_DIM)
    v = z[..., 2 * ATTN_WIDTH:3 * ATTN_WIDTH].reshape(B, T, N_HEADS, HEAD_DIM)
    u = z[..., 3 * ATTN_WIDTH:]
    q = rmsnorm(q, q_norm) * (HEAD_DIM ** -0.5)
    k = rmsnorm(k, k_norm)
    attn = neighbourhood_attention(q, k, v, rel_pos_bias)
    pool = multiscale_pool(u, w_pool, pool_scale)
    x = x + jnp.matmul(jnp.concatenate([attn, pool], axis=-1), w_out)
    x = x + expert_choice_ffn(rmsnorm(x, norm_ffn), w_router, w_gate, w_up, w_down)
    return x


def encoder_trunk(x, norm_mix, w_in, q_norm, k_norm, rel_pos_bias, w_pool, pool_scale,
                  w_out, norm_ffn, w_router, w_gate, w_up, w_down):
    for l in range(DEPTH):
        x = encoder_layer(x, norm_mix[l], w_in[l], q_norm[l], k_norm[l], rel_pos_bias[l],
                          w_pool[l], pool_scale[l], w_out[l], norm_ffn[l], w_router[l],
                          w_gate[l], w_up[l], w_down[l])
    return x


def setup_inputs(seed: int = 0) -> dict:
    key = jax.random.key(seed)
    ks = jax.random.split(key, 15)
    nrm = jax.random.normal
    f32 = jnp.float32
    L = DEPTH
    return {
        "x_prompt": nrm(ks[0], (BATCH, SEQ, D_MODEL), f32),
        "x_sample": nrm(ks[1], (DEC_BATCH, DEC_SEQ, D_MODEL), f32),
        "norm_mix": 1.0 + 0.1 * nrm(ks[2], (L, D_MODEL), f32),
        "w_in": nrm(ks[3], (L, D_MODEL, IN_WIDTH), f32) * D_MODEL ** -0.5,
        "q_norm": 1.0 + 0.1 * nrm(ks[4], (L, HEAD_DIM), f32),
        "k_norm": 1.0 + 0.1 * nrm(ks[5], (L, HEAD_DIM), f32),
        "rel_pos_bias": 0.1 * nrm(ks[6], (L, N_HEADS, 2 * WIN_R - 1, 2 * WIN_C - 1), f32),
        "w_pool": nrm(ks[7], (L, N_POOL_GROUPS, POOL_CH, POOL_CH), f32) * POOL_CH ** -0.5,
        "pool_scale": 1.0 + 0.1 * nrm(ks[8], (L, POOL_WIDTH), f32),
        "w_out": nrm(ks[9], (L, MIX_WIDTH, D_MODEL), f32) * MIX_WIDTH ** -0.5,
        "norm_ffn": 1.0 + 0.1 * nrm(ks[10], (L, D_MODEL), f32),
        "w_router": nrm(ks[11], (L, D_MODEL, N_EXPERTS), f32) * D_MODEL ** -0.5,
        "w_gate": nrm(ks[12], (L, N_EXPERTS, D_MODEL, EXPERT_FF), f32) * D_MODEL ** -0.5,
        "w_up": nrm(ks[13], (L, N_EXPERTS, D_MODEL, EXPERT_FF), f32) * D_MODEL ** -0.5,
        "w_down": nrm(ks[14], (L, N_EXPERTS, EXPERT_FF, D_MODEL), f32) * EXPERT_FF ** -0.5,
    }


def reference(x_prompt, x_sample, norm_mix, w_in, q_norm, k_norm, rel_pos_bias, w_pool,
              pool_scale, w_out, norm_ffn, w_router, w_gate, w_up, w_down):
    y_prompt = encoder_trunk(x_prompt, norm_mix, w_in, q_norm, k_norm, rel_pos_bias, w_pool,
                             pool_scale, w_out, norm_ffn, w_router, w_gate, w_up, w_down)
    y_sample = encoder_trunk(x_sample, norm_mix, w_in, q_norm, k_norm, rel_pos_bias, w_pool,
                             pool_scale, w_out, norm_ffn, w_router, w_gate, w_up, w_down)
    return (y_prompt, y_sample)
```

```python
import functools

import jax
import jax.numpy as jnp
from jax import lax
from jax.experimental import pallas as pl
from jax.experimental.pallas import tpu as pltpu

F32 = jnp.float32
BF16 = jnp.bfloat16
I32 = jnp.int32
U32 = jnp.uint32

GRID_W = 64
WIN_R = 8
WIN_C = 16
N_HEADS = 8
POOL_WINDOWS = (2, 4, 8, 16)
N_EXPERTS = 16
CAPACITY_FACTOR = 2
EPS = 1e-6

LANES = 128
SUBLANES = 8
MASKED = -1e30
VMEM_LIMIT = 56 * 1024 * 1024
HI16 = 0xFFFF0000

ROWS_PER_STEP = 4
POOL_HALO = 8
IN_PROJ_TOKENS = 512
EXPERT_TOKENS = 1024
EXPERT_FF_TILE = 256
GATHER_BLOCKS = 32
COMBINE_TOKENS = 128


def _pack_bf16_pairs(x):
    b = lax.bitcast_convert_type(x, U32)
    h = x.shape[1] // 2
    return b[:, :h] | (b[:, h:] >> 16)


def _unpack_bf16_pairs(w):
    hi = lax.bitcast_convert_type(w & jnp.uint32(HI16), F32)
    lo = lax.bitcast_convert_type(w << 16, F32)
    return hi, lo


def _in_proj_body(x_ref, g_ref, w_ref, qn_ref, kn_ref, q_ref, k_ref, v_ref, u_ref, *, attn, head_dim):
    x = x_ref[...]
    h = x * lax.rsqrt(jnp.mean(x * x, axis=-1, keepdims=True) + EPS) * g_ref[...]
    h = h.astype(BF16)

    def head_norm(z, gain, scale, out_ref):
        for hd in range(attn // head_dim):
            sl = slice(hd * head_dim, (hd + 1) * head_dim)
            zh = z[:, sl]
            r = lax.rsqrt(jnp.mean(zh * zh, axis=-1, keepdims=True) + EPS)
            out_ref[:, sl] = (zh * r * gain * scale).astype(BF16)

    zq = jnp.dot(h, w_ref[:, 0:attn], preferred_element_type=F32)
    head_norm(zq, qn_ref[...], head_dim ** -0.5, q_ref)
    zk = jnp.dot(h, w_ref[:, attn:2 * attn], preferred_element_type=F32)
    head_norm(zk, kn_ref[...], 1.0, k_ref)
    v_ref[...] = jnp.dot(h, w_ref[:, 2 * attn:3 * attn], preferred_element_type=F32).astype(BF16)
    u_ref[...] = jnp.dot(h, w_ref[:, 3 * attn:], preferred_element_type=F32)


def _in_proj(x, gain, w_in, q_norm, k_norm, *, attn, head_dim):
    n, d = x.shape
    width = w_in.shape[1]
    pool = width - 3 * attn
    tm = min(IN_PROJ_TOKENS, n)
    const = lambda i: (0, 0)
    tok = lambda i: (i, 0)
    return pl.pallas_call(
        functools.partial(_in_proj_body, attn=attn, head_dim=head_dim),
        grid=(n // tm,),
        in_specs=[
            pl.BlockSpec((tm, d), tok),
            pl.BlockSpec((1, d), const),
            pl.BlockSpec((d, width), const, pipeline_mode=pl.Buffered(1)),
            pl.BlockSpec((1, head_dim), const),
            pl.BlockSpec((1, head_dim), const),
        ],
        out_specs=[
            pl.BlockSpec((tm, attn), tok),
            pl.BlockSpec((tm, attn), tok),
            pl.BlockSpec((tm, attn), tok),
            pl.BlockSpec((tm, pool), tok),
        ],
        out_shape=[
            jax.ShapeDtypeStruct((n, attn), BF16),
            jax.ShapeDtypeStruct((n, attn), BF16),
            jax.ShapeDtypeStruct((n, attn), BF16),
            jax.ShapeDtypeStruct((n, pool), F32),
        ],
        compiler_params=pltpu.CompilerParams(
            dimension_semantics=("parallel",), vmem_limit_bytes=VMEM_LIMIT),
        name="in_proj",
    )(x, gain, w_in, q_norm, k_norm)


def _bias_tables(rpb):
    rq = ROWS_PER_STEP
    nk = 3 * rq
    qi = jnp.arange(rq)[:, None]
    kj = jnp.arange(nk)[None, :]
    d = kj - rq - qi
    dr = jnp.clip(d + WIN_R - 1, 0, 2 * WIN_R - 2)
    ones = jnp.ones((rq, nk), bool)
    valid_first = ones & (kj >= rq) & (kj - rq < WIN_R)
    valid_mid = (d >= -(WIN_R // 2)) & (d < WIN_R // 2)
    valid_last = ones & (kj < 2 * rq)
    c = jnp.arange(GRID_W)[:, None]
    kc = jnp.arange(GRID_W)[None, :]
    cs = jnp.clip(c - WIN_C // 2, 0, GRID_W - WIN_C)
    valid_c = (kc >= cs) & (kc < cs + WIN_C)
    dc = jnp.clip(kc - c + WIN_C - 1, 0, 2 * WIN_C - 2)
    vals = rpb[:, dr[:, None, :, None], dc[None, :, None, :]]
    tabs = []
    for vr in (valid_first, valid_mid, valid_last):
        ok = vr[:, None, :, None] & valid_c[None, :, None, :]
        t = jnp.where(ok[None], vals, MASKED)
        tabs.append(t.reshape(rpb.shape[0], rq * GRID_W, nk * GRID_W))
    return jnp.stack(tabs).astype(F32)


def _mix_body(q_ref, kp_ref, kc_ref, kn_ref, vp_ref, vc_ref, vn_ref, bt_ref, uc_ref, up_ref, un_ref, x_ref,
              wpool_ref, pscale_ref, wout_ref, nffn_ref, wr_ref, wr0_ref,
              x2_ref, h2p_ref, lg_ref, mix_ref, *, seq_len, head_dim, n_heads, pool_ch):
    i = pl.program_id(1)
    nb = pl.num_programs(1)
    tq = q_ref.shape[0]
    attn = n_heads * head_dim

    kcat = jnp.concatenate([kp_ref[...], kc_ref[...], kn_ref[...]], axis=0)
    vcat = jnp.concatenate([vp_ref[...], vc_ref[...], vn_ref[...]], axis=0)
    for h in range(n_heads):
        sl = slice(h * head_dim, (h + 1) * head_dim)
        s = lax.dot_general(q_ref[:, sl], kcat[:, sl], (((1,), (1,)), ((), ())),
                            preferred_element_type=F32)
        s = s + bt_ref[h]
        m = jnp.max(s, axis=-1, keepdims=True)
        p = jnp.exp(s - m)
        l = jnp.sum(p, axis=-1, keepdims=True)
        o = jnp.dot(p.astype(BF16), vcat[:, sl], preferred_element_type=F32)
        mix_ref[:, sl] = (o / l).astype(BF16)

    ucur = uc_ref[...]
    up = jnp.where(i == 0, 0.0, up_ref[...])
    un = jnp.where(i == nb - 1, 0.0, un_ref[...])
    uext = jnp.concatenate([up, ucur, un], axis=0)
    t = i * tq + lax.broadcasted_iota(I32, (tq, 1), 0)
    for g, w in enumerate(POOL_WINDOWS):
        gs = slice(g * pool_ch, (g + 1) * pool_ch)
        ug = uext[:, gs]
        arr = ug[:-1] + ug[1:]
        sh = 2
        while sh < w:
            arr = arr[:-sh] + arr[sh:]
            sh *= 2
        start = POOL_HALO - w // 2
        wsum = arr[start:start + tq]
        cnt = (jnp.minimum(t + w // 2, seq_len) - jnp.maximum(t - w // 2, 0)).astype(F32)
        pg = (wsum / cnt - ucur[:, gs]).astype(BF16)
        yg = jnp.dot(pg, wpool_ref[g], preferred_element_type=F32) * pscale_ref[:, gs]
        mix_ref[:, attn + g * pool_ch:attn + (g + 1) * pool_ch] = yg.astype(BF16)

    x2 = x_ref[...] + jnp.dot(mix_ref[...], wout_ref[...], preferred_element_type=F32)
    x2_ref[...] = x2
    hn = x2 * lax.rsqrt(jnp.mean(x2 * x2, axis=-1, keepdims=True) + EPS) * nffn_ref[...]
    hi = hn.astype(BF16)
    hi32 = hi.astype(F32)
    lo = (hn - hi32).astype(BF16)
    r = (jnp.dot(hi, wr_ref[...], preferred_element_type=F32)
         + jnp.dot(lo, wr0_ref[...], preferred_element_type=F32))
    lg_ref[...] = r + pltpu.roll(r, LANES - N_EXPERTS, 1)
    h2p_ref[...] = _pack_bf16_pairs(hi32)


def _mix(q, k, v, u, x, bias_tabs, w_pool, pool_scale, w_out, norm_ffn, wr, wr0, *, head_dim):
    b, t, attn = q.shape
    d = x.shape[-1]
    pool = u.shape[-1]
    n_heads = attn // head_dim
    pool_ch = pool // len(POOL_WINDOWS)
    tq = ROWS_PER_STEP * GRID_W
    nb = t // tq
    assert 2 * ROWS_PER_STEP == WIN_R and t % tq == 0 and nb >= 2
    hb = tq // POOL_HALO

    cur = lambda bi, i: (bi, i, 0)
    prev = lambda bi, i: (bi, jnp.maximum(i - 1, 0), 0)
    nxt = lambda bi, i: (bi, jnp.minimum(i + 1, nb - 1), 0)
    c2 = lambda bi, i: (0, 0)
    c3 = lambda bi, i: (0, 0, 0)
    kv = lambda im: pl.BlockSpec((None, tq, attn), im)
    return pl.pallas_call(
        functools.partial(_mix_body, seq_len=t, head_dim=head_dim, n_heads=n_heads, pool_ch=pool_ch),
        grid=(b, nb),
        in_specs=[
            kv(cur), kv(prev), kv(cur), kv(nxt), kv(prev), kv(cur), kv(nxt),
            pl.BlockSpec((None, n_heads, tq, 3 * tq),
                         lambda bi, i: (jnp.where(i == 0, 0, jnp.where(i == nb - 1, 2, 1)), 0, 0, 0)),
            pl.BlockSpec((None, tq, pool), cur),
            pl.BlockSpec((None, POOL_HALO, pool), lambda bi, i: (bi, jnp.maximum(i * hb - 1, 0), 0)),
            pl.BlockSpec((None, POOL_HALO, pool),
                         lambda bi, i: (bi, jnp.minimum((i + 1) * hb, t // POOL_HALO - 1), 0)),
            pl.BlockSpec((None, tq, d), cur),
            pl.BlockSpec(w_pool.shape, c3),
            pl.BlockSpec((1, pool), c2),
            pl.BlockSpec(w_out.shape, c2, pipeline_mode=pl.Buffered(1)),
            pl.BlockSpec((1, d), c2),
            pl.BlockSpec(wr.shape, c2),
            pl.BlockSpec(wr0.shape, c2),
        ],
        out_specs=[
            pl.BlockSpec((None, tq, d), cur),
            pl.BlockSpec((None, tq, d // 2), cur),
            pl.BlockSpec((None, tq, LANES), cur),
        ],
        out_shape=[
            jax.ShapeDtypeStruct((b, t, d), F32),
            jax.ShapeDtypeStruct((b, t, d // 2), U32),
            jax.ShapeDtypeStruct((b, t, LANES), F32),
        ],
        scratch_shapes=[pltpu.VMEM((tq, attn + pool), BF16)],
        compiler_params=pltpu.CompilerParams(
            dimension_semantics=("parallel", "arbitrary"), vmem_limit_bytes=VMEM_LIMIT),
        name="mix",
    )(q, k, k, k, v, v, v, bias_tabs, u, u, u, x, w_pool, pool_scale, w_out, norm_ffn, wr, wr0)


def _route_body(lg_ref, l_ref, cnt_ref, base_ref, src_ref, wt_ref, cntt_ref, aff_ref, *, cap):
    n_exp, tl, nblk = lg_ref.shape

    m = lg_ref[0]
    for e in range(1, n_exp):
        m = jnp.maximum(m, lg_ref[e])
    den = jnp.zeros_like(m)
    for e in range(n_exp):
        ex = jnp.exp(lg_ref[e] - m)
        aff_ref[e] = ex
        den = den + ex
    for e in range(n_exp):
        aff_ref[e] = aff_ref[e] / den

    def total(x):
        return jnp.sum(jnp.sum(x, axis=0, keepdims=True), axis=1, keepdims=True)

    def search(it, thr):
        bit = jnp.left_shift(jnp.int32(1), 30 - it)
        out = []
        for e in range(n_exp):
            bits = lax.bitcast_convert_type(aff_ref[e], I32)
            cand = thr[e] | bit
            c = total(jnp.where(bits >= cand, 1.0, 0.0))
            out.append(jnp.where(c >= cap, cand, thr[e]))
        return tuple(out)

    thr = lax.fori_loop(0, 31, search, tuple(jnp.zeros((1, 1), I32) for _ in range(n_exp)))

    r_i = lax.broadcasted_iota(I32, (tl, tl), 0)
    c_i = lax.broadcasted_iota(I32, (tl, tl), 1)
    tri = jnp.where(c_i <= r_i, 1.0, 0.0).astype(BF16)
    rb = lax.broadcasted_iota(I32, (nblk, nblk), 0)
    cb = lax.broadcasted_iota(I32, (nblk, nblk), 1)
    before = jnp.where(rb < cb, 1.0, 0.0).astype(BF16)
    blk_tok = lax.broadcasted_iota(I32, (1, nblk), 1) * tl

    def prefix(mask):
        incl = jnp.dot(tri, mask.astype(BF16), preferred_element_type=F32)
        tot = incl[tl - 1:tl, :]
        tot8 = jnp.broadcast_to(tot, (SUBLANES, nblk)).astype(BF16)
        base = jnp.dot(tot8, before, preferred_element_type=F32)[0:1, :]
        return incl, tot, base

    for r in range(n_exp):
        src_ref[r] = jnp.zeros((tl, nblk), I32)
        wt_ref[r] = jnp.zeros((tl, nblk), F32)

    rank = jnp.zeros((tl, nblk), F32)
    for e in range(n_exp):
        aff = aff_ref[e]
        bits = lax.bitcast_convert_type(aff, I32)
        gt = bits > thr[e]
        eq = bits == thr[e]
        need = cap - total(jnp.where(gt, 1.0, 0.0))
        eqf = jnp.where(eq, 1.0, 0.0)
        incl_eq, _, base_eq = prefix(eqf)
        eq_rank = base_eq + incl_eq - eqf
        sel = gt | (eq & (eq_rank < need))
        self_ = jnp.where(sel, 1.0, 0.0)
        incl, tot, base = prefix(self_)
        slot = base + incl - 1.0

        cnt_ref[e] = jnp.broadcast_to(tot, (SUBLANES, nblk)).astype(I32)
        base_ref[e] = jnp.broadcast_to(base, (SUBLANES, nblk)).astype(I32)

        def nth(j, carry, incl=incl, e=e):
            pos = jnp.sum(jnp.where(incl <= jnp.asarray(j, F32), 1.0, 0.0), axis=0, keepdims=True)
            l_ref[e, pl.ds(j, 1), :] = pos.astype(I32) + blk_tok
            return carry

        lax.fori_loop(0, tl, nth, 0)

        row = (e * cap + slot).astype(I32)
        for r in range(e + 1):
            hit = sel & (rank == r)
            src_ref[r] = jnp.where(hit, row, src_ref[r])
            wt_ref[r] = jnp.where(hit, aff, wt_ref[r])
        rank = rank + self_
    cntt_ref[...] = rank.astype(I32)


def _route(lg, cap):
    n_exp, tl, nblk = lg.shape
    full = lambda: pl.BlockSpec(memory_space=pltpu.VMEM)
    return pl.pallas_call(
        functools.partial(_route_body, cap=cap),
        in_specs=[full()],
        out_specs=[full() for _ in range(6)],
        out_shape=[
            jax.ShapeDtypeStruct((n_exp, tl, nblk), I32),
            jax.ShapeDtypeStruct((n_exp, SUBLANES, nblk), I32),
            jax.ShapeDtypeStruct((n_exp, SUBLANES, nblk), I32),
            jax.ShapeDtypeStruct((n_exp, tl, nblk), I32),
            jax.ShapeDtypeStruct((n_exp, tl, nblk), F32),
            jax.ShapeDtypeStruct((tl, nblk), I32),
        ],
        scratch_shapes=[pltpu.VMEM((n_exp, tl, nblk), F32)],
        compiler_params=pltpu.CompilerParams(vmem_limit_bytes=VMEM_LIMIT),
        name="route",
    )(lg)


def _row_copy(src_ref, src_row, dst_ref, dst_row, sem):
    return pltpu.make_async_copy(src_ref.at[pl.ds(src_row, 1)], dst_ref.at[pl.ds(dst_row, 1)], sem)


def _gather_body(l_ref, cnt_ref, base_ref, h_ref, xe_ref, sem, *, cap, nblocks):
    e = pl.program_id(0)

    def block(b, issued):
        n = cnt_ref[0, b]
        first = e * cap + base_ref[0, b]

        def row(j, carry):
            _row_copy(h_ref, l_ref[0, b * LANES + j], xe_ref, first + j, sem).start()
            return carry

        lax.fori_loop(0, n, row, 0)
        return issued + n

    issued = lax.fori_loop(0, nblocks, block, 0)

    def drain(j, carry):
        _row_copy(h_ref, 0, xe_ref, 0, sem).wait()
        return carry

    lax.fori_loop(0, issued, drain, 0)


def _gather(lists, cnt, base, h2p, cap):
    groups = lists.shape[0]
    chunks = groups // N_EXPERTS
    nblocks = cnt.shape[-1]
    smem = lambda a: pl.BlockSpec((None, 1, a.shape[-1]), lambda e, c: (e * chunks + c, 0, 0),
                                  memory_space=pltpu.SMEM)
    return pl.pallas_call(
        functools.partial(_gather_body, cap=cap, nblocks=nblocks),
        grid=(N_EXPERTS, chunks),
        in_specs=[smem(lists), smem(cnt), smem(base), pl.BlockSpec(memory_space=pl.ANY)],
        out_specs=pl.BlockSpec(memory_space=pl.ANY),
        out_shape=jax.ShapeDtypeStruct((N_EXPERTS * cap, h2p.shape[1]), U32),
        scratch_shapes=[pltpu.SemaphoreType.DMA(())],
        compiler_params=pltpu.CompilerParams(
            dimension_semantics=("arbitrary", "arbitrary"), has_side_effects=True),
        name="gather",
    )(lists, cnt, base, h2p)


def _expert_body(xe_ref, wg_ref, wu_ref, wd_ref, y_ref, x_scr, acc_ref):
    f = pl.program_id(2)
    half = xe_ref.shape[1]

    @pl.when(f == 0)
    def _():
        hi, lo = _unpack_bf16_pairs(xe_ref[...])
        x_scr[:, :half] = hi.astype(BF16)
        x_scr[:, half:] = lo.astype(BF16)
        acc_ref[...] = jnp.zeros_like(acc_ref)

    x = x_scr[...]
    a = jnp.dot(x, wg_ref[...].astype(BF16), preferred_element_type=F32)
    b = jnp.dot(x, wu_ref[...].astype(BF16), preferred_element_type=F32)
    hmid = (a * jax.nn.sigmoid(a) * b).astype(BF16)
    acc_ref[...] += jnp.dot(hmid, wd_ref[...].astype(BF16), preferred_element_type=F32)

    @pl.when(f == pl.num_programs(2) - 1)
    def _():
        y_ref[...] = _pack_bf16_pairs(acc_ref[...].astype(BF16).astype(F32))


def _experts(xe, w_gate, w_up, w_down, cap):
    n_exp, d, ff = w_gate.shape
    half = xe.shape[1]
    tc = min(EXPERT_TOKENS, cap)
    tf = EXPERT_FF_TILE
    ct = cap // tc
    assert cap % tc == 0 and ff % tf == 0
    return pl.pallas_call(
        _expert_body,
        grid=(n_exp, ct, ff // tf),
        in_specs=[
            pl.BlockSpec((tc, half), lambda e, c, f: (e * ct + c, 0)),
            pl.BlockSpec((None, d, tf), lambda e, c, f: (e, 0, f)),
            pl.BlockSpec((None, d, tf), lambda e, c, f: (e, 0, f)),
            pl.BlockSpec((None, tf, d), lambda e, c, f: (e, f, 0)),
        ],
        out_specs=pl.BlockSpec((tc, half), lambda e, c, f: (e * ct + c, 0)),
        out_shape=jax.ShapeDtypeStruct((n_exp * cap, half), U32),
        scratch_shapes=[pltpu.VMEM((tc, d), BF16), pltpu.VMEM((tc, d), F32)],
        compiler_params=pltpu.CompilerParams(
            dimension_semantics=("parallel", "parallel", "arbitrary"), vmem_limit_bytes=VMEM_LIMIT),
        name="experts",
    )(xe, w_gate, w_up, w_down)


def _combine_body(cnt_ref, src_ref, x2_ref, wt_ref, y_ref, o_ref, g_scr, sem):
    n_rank, tb, half = g_scr.shape

    @pl.when(pl.program_id(0) == 0)
    def _():
        g_scr[...] = jnp.zeros_like(g_scr)

    def token(t, carry):
        issued, deepest = carry
        n = cnt_ref[0, t]

        def rank(r, c):
            pltpu.make_async_copy(y_ref.at[pl.ds(src_ref[0, t * n_rank + r], 1)],
                                  g_scr.at[r, pl.ds(t, 1)], sem).start()
            return c

        lax.fori_loop(0, n, rank, 0)
        return issued + n, jnp.maximum(deepest, n)

    issued, deepest = lax.fori_loop(0, tb, token, (0, 0))

    def drain(j, carry):
        pltpu.make_async_copy(y_ref.at[pl.ds(0, 1)], g_scr.at[0, pl.ds(0, 1)], sem).wait()
        return carry

    lax.fori_loop(0, issued, drain, 0)

    o_ref[...] = x2_ref[...]
    for r in range(n_rank):
        @pl.when(r < deepest)
        def _(r=r):
            w = wt_ref[:, r:r + 1]
            hi, lo = _unpack_bf16_pairs(g_scr[r])
            live = w > 0.0
            o_ref[:, :half] += jnp.where(live, w * hi, 0.0)
            o_ref[:, half:] += jnp.where(live, w * lo, 0.0)


def _combine(cntt, src, x2, wt, y):
    n, d = x2.shape
    tb = cntt.shape[-1]
    smem = lambda a: pl.BlockSpec((None, 1, a.shape[-1]), lambda i: (i, 0, 0), memory_space=pltpu.SMEM)
    return pl.pallas_call(
        _combine_body,
        grid=(n // tb,),
        in_specs=[
            smem(cntt), smem(src),
            pl.BlockSpec((tb, d), lambda i: (i, 0)),
            pl.BlockSpec((tb, N_EXPERTS), lambda i: (i, 0)),
            pl.BlockSpec(memory_space=pl.ANY),
        ],
        out_specs=pl.BlockSpec((tb, d), lambda i: (i, 0)),
        out_shape=jax.ShapeDtypeStruct((n, d), F32),
        scratch_shapes=[pltpu.VMEM((N_EXPERTS, tb, d // 2), U32), pltpu.SemaphoreType.DMA(())],
        compiler_params=pltpu.CompilerParams(
            dimension_semantics=("arbitrary",), vmem_limit_bytes=VMEM_LIMIT),
        name="combine",
    )(cntt, src, x2, wt, y)


def _layer(x, p):
    b, t, d = x.shape
    n = b * t
    head_dim = p["q_norm"].shape[-1]
    attn = N_HEADS * head_dim
    q, k, v, u = _in_proj(x.reshape(n, d), p["norm_mix"], p["w_in"], p["q_norm"], p["k_norm"],
                          attn=attn, head_dim=head_dim)
    shp = lambda a: a.reshape(b, t, a.shape[-1])
    x2, h2p, lg = _mix(shp(q), shp(k), shp(v), shp(u), x, p["bias_tabs"], p["w_pool"], p["pool_scale"],
                       p["w_out"], p["norm_ffn"], p["wr"], p["wr0"], head_dim=head_dim)

    nblk = n // LANES
    cap = max(1, (CAPACITY_FACTOR * n) // N_EXPERTS)
    lg = lg.reshape(n, LANES)[:, :N_EXPERTS]
    lg = lg.T.reshape(N_EXPERTS, nblk, LANES).transpose(0, 2, 1)
    lists, cnt, base, src, wt, cntt = _route(lg, cap)

    gb = min(GATHER_BLOCKS, nblk)
    chunks = nblk // gb
    lists = lists.transpose(0, 2, 1).reshape(N_EXPERTS * chunks, 1, gb * LANES)
    cnt = cnt[:, 0, :].reshape(N_EXPERTS * chunks, 1, gb)
    base = base[:, 0, :].reshape(N_EXPERTS * chunks, 1, gb)
    xe = _gather(lists, cnt, base, h2p.reshape(n, d // 2), cap)

    y = _experts(xe, p["w_gate"], p["w_up"], p["w_down"], cap)

    tb = min(COMBINE_TOKENS, n)
    src = src.transpose(2, 1, 0).reshape(n // tb, 1, tb * N_EXPERTS)
    wt = wt.transpose(2, 1, 0).reshape(n, N_EXPERTS)
    cntt = cntt.T.reshape(n // tb, 1, tb)
    out = _combine(cntt, src, x2.reshape(n, d), wt, y)
    return out.reshape(b, t, d)


def _prepare(l, norm_mix, w_in, q_norm, k_norm, rel_pos_bias, w_pool, pool_scale, w_out, norm_ffn,
             w_router, w_gate, w_up, w_down):
    d = w_in.shape[1]
    whi = w_router[l].astype(BF16)
    wlo = (w_router[l] - whi.astype(F32)).astype(BF16)
    pad = lambda *cols: jnp.concatenate(
        list(cols) + [jnp.zeros((d, LANES - sum(c.shape[1] for c in cols)), BF16)], axis=1)
    return dict(
        norm_mix=norm_mix[l][None, :], w_in=w_in[l].astype(BF16),
        q_norm=q_norm[l][None, :], k_norm=k_norm[l][None, :],
        bias_tabs=_bias_tables(rel_pos_bias[l]), w_pool=w_pool[l].astype(BF16),
        pool_scale=pool_scale[l][None, :], w_out=w_out[l].astype(BF16), norm_ffn=norm_ffn[l][None, :],
        wr=pad(whi, wlo), wr0=pad(whi),
        w_gate=w_gate[l], w_up=w_up[l], w_down=w_down[l],
    )


def kernel(x_prompt, x_sample, norm_mix, w_in, q_norm, k_norm, rel_pos_bias, w_pool, pool_scale, w_out,
           norm_ffn, w_router, w_gate, w_up, w_down):
    weights = (norm_mix, w_in, q_norm, k_norm, rel_pos_bias, w_pool, pool_scale, w_out, norm_ffn,
               w_router, w_gate, w_up, w_down)
    layers = [_prepare(l, *weights) for l in range(w_in.shape[0])]

    def trunk(x):
        for p in layers:
            x = _layer(x, p)
        return x

    return trunk(x_prompt), trunk(x_sample)
```

```python
import functools

import numpy as np
import jax
import jax.numpy as jnp
from jax import lax
from jax.experimental import pallas as pl
from jax.experimental.pallas import tpu as pltpu

F32 = jnp.float32
BF16 = jnp.bfloat16
I32 = jnp.int32
U32 = jnp.uint32

GRID_W = 64
WIN_R = 8
WIN_C = 16
N_HEADS = 8
POOL_WINDOWS = (2, 4, 8, 16)
N_EXPERTS = 16
CAPACITY_FACTOR = 2
EPS = 1e-6

LANES = 128
SUBLANES = 8
MASKED = -1e30
VMEM_LIMIT = 56 * 1024 * 1024
HI16 = 0xFFFF0000

ROWS_PER_STEP = 4
POOL_HALO = 8
IN_PROJ_TOKENS = 512
EXPERT_TOKENS = 1024
EXPERT_FF_TILE = 256
GATHER_STEPS = 8
GATHER_UNROLL = 8
COMBINE_TOKENS = 128


def _pack_bf16_pairs(x):
    b = lax.bitcast_convert_type(x, U32)
    h = x.shape[1] // 2
    return b[:, :h] | (b[:, h:] >> 16)


def _unpack_bf16_pairs(w):
    hi = lax.bitcast_convert_type(w & jnp.uint32(HI16), F32)
    lo = lax.bitcast_convert_type(w << 16, F32)
    return hi, lo


def _in_proj_body(x_ref, g_ref, w_ref, qn_ref, kn_ref, q_ref, k_ref, v_ref, u_ref, *, attn, head_dim):
    x = x_ref[...]
    h = x * lax.rsqrt(jnp.mean(x * x, axis=-1, keepdims=True) + EPS) * g_ref[...]
    h = h.astype(BF16)

    def head_norm(z, gain, scale, out_ref):
        for hd in range(attn // head_dim):
            sl = slice(hd * head_dim, (hd + 1) * head_dim)
            zh = z[:, sl]
            r = lax.rsqrt(jnp.mean(zh * zh, axis=-1, keepdims=True) + EPS)
            out_ref[:, sl] = (zh * r * gain * scale).astype(BF16)

    zq = jnp.dot(h, w_ref[:, 0:attn], preferred_element_type=F32)
    head_norm(zq, qn_ref[...], head_dim ** -0.5, q_ref)
    zk = jnp.dot(h, w_ref[:, attn:2 * attn], preferred_element_type=F32)
    head_norm(zk, kn_ref[...], 1.0, k_ref)
    v_ref[...] = jnp.dot(h, w_ref[:, 2 * attn:3 * attn], preferred_element_type=F32).astype(BF16)
    u_ref[...] = jnp.dot(h, w_ref[:, 3 * attn:], preferred_element_type=F32)


def _in_proj(x, gain, w_in, q_norm, k_norm, *, attn, head_dim):
    n, d = x.shape
    width = w_in.shape[1]
    pool = width - 3 * attn
    tm = min(IN_PROJ_TOKENS, n)
    const = lambda i: (0, 0)
    tok = lambda i: (i, 0)
    return pl.pallas_call(
        functools.partial(_in_proj_body, attn=attn, head_dim=head_dim),
        grid=(n // tm,),
        in_specs=[
            pl.BlockSpec((tm, d), tok),
            pl.BlockSpec((1, d), const),
            pl.BlockSpec((d, width), const, pipeline_mode=pl.Buffered(1)),
            pl.BlockSpec((1, head_dim), const),
            pl.BlockSpec((1, head_dim), const),
        ],
        out_specs=[
            pl.BlockSpec((tm, attn), tok),
            pl.BlockSpec((tm, attn), tok),
            pl.BlockSpec((tm, attn), tok),
            pl.BlockSpec((tm, pool), tok),
        ],
        out_shape=[
            jax.ShapeDtypeStruct((n, attn), BF16),
            jax.ShapeDtypeStruct((n, attn), BF16),
            jax.ShapeDtypeStruct((n, attn), BF16),
            jax.ShapeDtypeStruct((n, pool), F32),
        ],
        compiler_params=pltpu.CompilerParams(
            dimension_semantics=("parallel",), vmem_limit_bytes=VMEM_LIMIT),
        name="in_proj",
    )(x, gain, w_in, q_norm, k_norm)


def _bias_tables(rpb):
    rq = ROWS_PER_STEP
    nk = 3 * rq
    qi = np.arange(rq)[:, None]
    kj = np.arange(nk)[None, :]
    d = kj - rq - qi
    dr = np.clip(d + WIN_R - 1, 0, 2 * WIN_R - 2)
    ones = np.ones((rq, nk), bool)
    valid_first = ones & (kj >= rq) & (kj - rq < WIN_R)
    valid_mid = (d >= -(WIN_R // 2)) & (d < WIN_R // 2)
    valid_last = ones & (kj < 2 * rq)
    c = np.arange(GRID_W)[:, None]
    kc = np.arange(GRID_W)[None, :]
    cs = np.clip(c - WIN_C // 2, 0, GRID_W - WIN_C)
    valid_c = (kc >= cs) & (kc < cs + WIN_C)
    padded = jnp.pad(rpb, ((0, 0), (0, 0), (GRID_W, GRID_W)))
    first = GRID_W + WIN_C - 1
    by_col = jnp.stack([padded[:, :, first - ci:first - ci + GRID_W] for ci in range(GRID_W)],
                       axis=2)
    vals = jnp.stack([jnp.stack([by_col[:, dr[a, b]] for b in range(nk)], axis=2) for a in range(rq)],
                     axis=1)
    tabs = []
    for vr in (valid_first, valid_mid, valid_last):
        ok = vr[:, None, :, None] & valid_c[None, :, None, :]
        t = jnp.where(ok[None], vals, MASKED)
        tabs.append(t.reshape(rpb.shape[0], rq * GRID_W, nk * GRID_W))
    return jnp.stack(tabs).astype(F32)


def _mix_body(q_ref, kp_ref, kc_ref, kn_ref, vp_ref, vc_ref, vn_ref, bt_ref, uc_ref, up_ref, un_ref, x_ref,
              wpool_ref, pscale_ref, wout_ref, nffn_ref, wr_ref, wr0_ref,
              x2_ref, h2p_ref, lg_ref, mix_ref, *, seq_len, head_dim, n_heads, pool_ch):
    i = pl.program_id(1)
    nb = pl.num_programs(1)
    tq = q_ref.shape[0]
    attn = n_heads * head_dim

    kcat = jnp.concatenate([kp_ref[...], kc_ref[...], kn_ref[...]], axis=0)
    vcat = jnp.concatenate([vp_ref[...], vc_ref[...], vn_ref[...]], axis=0)
    for h in range(n_heads):
        sl = slice(h * head_dim, (h + 1) * head_dim)
        s = lax.dot_general(q_ref[:, sl], kcat[:, sl], (((1,), (1,)), ((), ())),
                            preferred_element_type=F32)
        s = s + bt_ref[h]
        m = jnp.max(s, axis=-1, keepdims=True)
        p = jnp.exp(s - m)
        l = jnp.sum(p, axis=-1, keepdims=True)
        o = jnp.dot(p.astype(BF16), vcat[:, sl], preferred_element_type=F32)
        mix_ref[:, sl] = (o / l).astype(BF16)

    ucur = uc_ref[...]
    up = jnp.where(i == 0, 0.0, up_ref[...])
    un = jnp.where(i == nb - 1, 0.0, un_ref[...])
    uext = jnp.concatenate([up, ucur, un], axis=0)
    t = i * tq + lax.broadcasted_iota(I32, (tq, 1), 0)
    for g, w in enumerate(POOL_WINDOWS):
        gs = slice(g * pool_ch, (g + 1) * pool_ch)
        ug = uext[:, gs]
        arr = ug[:-1] + ug[1:]
        sh = 2
        while sh < w:
            arr = arr[:-sh] + arr[sh:]
            sh *= 2
        start = POOL_HALO - w // 2
        wsum = arr[start:start + tq]
        cnt = (jnp.minimum(t + w // 2, seq_len) - jnp.maximum(t - w // 2, 0)).astype(F32)
        pg = (wsum / cnt - ucur[:, gs]).astype(BF16)
        yg = jnp.dot(pg, wpool_ref[g], preferred_element_type=F32) * pscale_ref[:, gs]
        mix_ref[:, attn + g * pool_ch:attn + (g + 1) * pool_ch] = yg.astype(BF16)

    x2 = x_ref[...] + jnp.dot(mix_ref[...], wout_ref[...], preferred_element_type=F32)
    x2_ref[...] = x2
    hn = x2 * lax.rsqrt(jnp.mean(x2 * x2, axis=-1, keepdims=True) + EPS) * nffn_ref[...]
    hi = hn.astype(BF16)
    hi32 = hi.astype(F32)
    lo = (hn - hi32).astype(BF16)
    r = (jnp.dot(hi, wr_ref[...], preferred_element_type=F32)
         + jnp.dot(lo, wr0_ref[...], preferred_element_type=F32))
    lg_ref[...] = r + pltpu.roll(r, LANES - N_EXPERTS, 1)
    h2p_ref[...] = _pack_bf16_pairs(hi32)


def _mix(q, k, v, u, x, bias_tabs, w_pool, pool_scale, w_out, norm_ffn, wr, wr0, *, head_dim):
    b, t, attn = q.shape
    d = x.shape[-1]
    pool = u.shape[-1]
    n_heads = attn // head_dim
    pool_ch = pool // len(POOL_WINDOWS)
    tq = ROWS_PER_STEP * GRID_W
    nb = t // tq
    assert 2 * ROWS_PER_STEP == WIN_R and t % tq == 0 and nb >= 2
    hb = tq // POOL_HALO

    cur = lambda bi, i: (bi, i, 0)
    prev = lambda bi, i: (bi, jnp.maximum(i - 1, 0), 0)
    nxt = lambda bi, i: (bi, jnp.minimum(i + 1, nb - 1), 0)
    c2 = lambda bi, i: (0, 0)
    c3 = lambda bi, i: (0, 0, 0)
    kv = lambda im: pl.BlockSpec((None, tq, attn), im)
    return pl.pallas_call(
        functools.partial(_mix_body, seq_len=t, head_dim=head_dim, n_heads=n_heads, pool_ch=pool_ch),
        grid=(b, nb),
        in_specs=[
            kv(cur), kv(prev), kv(cur), kv(nxt), kv(prev), kv(cur), kv(nxt),
            pl.BlockSpec((None, n_heads, tq, 3 * tq),
                         lambda bi, i: (jnp.where(i == 0, 0, jnp.where(i == nb - 1, 2, 1)), 0, 0, 0)),
            pl.BlockSpec((None, tq, pool), cur),
            pl.BlockSpec((None, POOL_HALO, pool), lambda bi, i: (bi, jnp.maximum(i * hb - 1, 0), 0)),
            pl.BlockSpec((None, POOL_HALO, pool),
                         lambda bi, i: (bi, jnp.minimum((i + 1) * hb, t // POOL_HALO - 1), 0)),
            pl.BlockSpec((None, tq, d), cur),
            pl.BlockSpec(w_pool.shape, c3),
            pl.BlockSpec((1, pool), c2),
            pl.BlockSpec(w_out.shape, c2, pipeline_mode=pl.Buffered(1)),
            pl.BlockSpec((1, d), c2),
            pl.BlockSpec(wr.shape, c2),
            pl.BlockSpec(wr0.shape, c2),
        ],
        out_specs=[
            pl.BlockSpec((None, tq, d), cur),
            pl.BlockSpec((None, tq, d // 2), cur),
            pl.BlockSpec((None, tq, LANES), cur),
        ],
        out_shape=[
            jax.ShapeDtypeStruct((b, t, d), F32),
            jax.ShapeDtypeStruct((b, t, d // 2), U32),
            jax.ShapeDtypeStruct((b, t, LANES), F32),
        ],
        scratch_shapes=[pltpu.VMEM((tq, attn + pool), BF16)],
        compiler_params=pltpu.CompilerParams(
            dimension_semantics=("parallel", "arbitrary"), vmem_limit_bytes=VMEM_LIMIT),
        name="mix",
    )(q, k, k, k, v, v, v, bias_tabs, u, u, u, x, w_pool, pool_scale, w_out, norm_ffn, wr, wr0)


def _route_body(lg_ref, idx_ref, src_ref, wt_ref, cntt_ref, aff_ref, *, cap):
    n_exp, tl, nblk = lg_ref.shape

    m = lg_ref[0]
    for e in range(1, n_exp):
        m = jnp.maximum(m, lg_ref[e])
    den = jnp.zeros_like(m)
    for e in range(n_exp):
        ex = jnp.exp(lg_ref[e] - m)
        aff_ref[e] = ex
        den = den + ex
    for e in range(n_exp):
        aff_ref[e] = aff_ref[e] / den

    def total(x):
        return jnp.sum(jnp.sum(x, axis=0, keepdims=True), axis=1, keepdims=True)

    def search(it, thr):
        bit = jnp.left_shift(jnp.int32(1), 30 - it)
        out = []
        for e in range(n_exp):
            bits = lax.bitcast_convert_type(aff_ref[e], I32)
            cand = thr[e] | bit
            c = total(jnp.where(bits >= cand, 1.0, 0.0))
            out.append(jnp.where(c >= cap, cand, thr[e]))
        return tuple(out)

    thr = lax.fori_loop(0, 31, search, tuple(jnp.zeros((1, 1), I32) for _ in range(n_exp)))

    r_i = lax.broadcasted_iota(I32, (tl, tl), 0)
    c_i = lax.broadcasted_iota(I32, (tl, tl), 1)
    tri = jnp.where(c_i <= r_i, 1.0, 0.0).astype(BF16)
    rb = lax.broadcasted_iota(I32, (nblk, nblk), 0)
    cb = lax.broadcasted_iota(I32, (nblk, nblk), 1)
    before = jnp.where(rb < cb, 1.0, 0.0).astype(BF16)
    after = jnp.where(cb < rb, 1.0, 0.0).astype(BF16)
    slot_f = lax.broadcasted_iota(I32, (nblk, cap), 1).astype(F32)
    blk_f = lax.broadcasted_iota(I32, (nblk, cap), 0).astype(F32)

    def prefix(mask):
        incl = jnp.dot(tri, mask.astype(BF16), preferred_element_type=F32)
        tot8 = jnp.broadcast_to(incl[tl - 1:tl, :], (SUBLANES, nblk)).astype(BF16)
        base = jnp.dot(tot8, before, preferred_element_type=F32)[0:1, :]
        return incl, base

    for r in range(n_exp):
        src_ref[r] = jnp.zeros((tl, nblk), I32)
        wt_ref[r] = jnp.zeros((tl, nblk), F32)

    rank = jnp.zeros((tl, nblk), F32)
    for e in range(n_exp):
        aff = aff_ref[e]
        bits = lax.bitcast_convert_type(aff, I32)
        gt = bits > thr[e]
        eq = bits == thr[e]
        need = cap - total(jnp.where(gt, 1.0, 0.0))
        eqf = jnp.where(eq, 1.0, 0.0)
        incl_eq, base_eq = prefix(eqf)
        eq_rank = base_eq + incl_eq - eqf
        sel = gt | (eq & (eq_rank < need))
        self_ = jnp.where(sel, 1.0, 0.0)
        incl, base = prefix(self_)
        slot = base + incl - 1.0

        tot_c = jnp.sum(self_.T, axis=1, keepdims=True)
        base_c = jnp.dot(after, jnp.broadcast_to(tot_c, (nblk, LANES)).astype(BF16),
                         preferred_element_type=F32)[:, 0:1]
        owns = (base_c <= slot_f) & (slot_f < base_c + tot_c)
        local = jnp.sum(jnp.where(owns, slot_f - base_c, 0.0), axis=0, keepdims=True)
        blk = jnp.sum(jnp.where(owns, blk_f, 0.0), axis=0, keepdims=True)
        incl_at = jnp.dot(incl.astype(BF16), jnp.where(owns, 1.0, 0.0).astype(BF16),
                          preferred_element_type=F32)
        pos = jnp.sum(jnp.where(incl_at <= local, 1.0, 0.0), axis=0, keepdims=True)
        idx_ref[e] = (blk * tl + pos).astype(I32)

        row = (e * cap + slot).astype(I32)
        for r in range(e + 1):
            hit = sel & (rank == r)
            src_ref[r] = jnp.where(hit, row, src_ref[r])
            wt_ref[r] = jnp.where(hit, aff, wt_ref[r])
        rank = rank + self_
    cntt_ref[...] = rank.astype(I32)


def _route(lg, cap):
    n_exp, tl, nblk = lg.shape
    full = lambda: pl.BlockSpec(memory_space=pltpu.VMEM)
    return pl.pallas_call(
        functools.partial(_route_body, cap=cap),
        in_specs=[full()],
        out_specs=[full() for _ in range(4)],
        out_shape=[
            jax.ShapeDtypeStruct((n_exp, 1, cap), I32),
            jax.ShapeDtypeStruct((n_exp, tl, nblk), I32),
            jax.ShapeDtypeStruct((n_exp, tl, nblk), F32),
            jax.ShapeDtypeStruct((tl, nblk), I32),
        ],
        scratch_shapes=[pltpu.VMEM((n_exp, tl, nblk), F32)],
        compiler_params=pltpu.CompilerParams(vmem_limit_bytes=VMEM_LIMIT),
        name="route",
    )(lg)


def _expert_body(idx_ref, nxt_ref, h_ref, wg_ref, wu_ref, wd_ref, y_ref, rows, x_scr, acc_ref, sem):
    e, c, f = pl.program_id(0), pl.program_id(1), pl.program_id(2)
    tile = e * pl.num_programs(1) + c
    n_tiles = pl.num_programs(0) * pl.num_programs(1)
    tc, half = rows.shape
    per_step = tc // GATHER_STEPS

    def fetch(tok_ref, first, count):
        def group(g, carry):
            for k in range(GATHER_UNROLL):
                r = first + g * GATHER_UNROLL + k
                pltpu.make_async_copy(h_ref.at[pl.ds(tok_ref[0, r], 1)], rows.at[pl.ds(r, 1)], sem).start()
            return carry
        lax.fori_loop(0, count // GATHER_UNROLL, group, 0)

    @pl.when((tile == 0) & (f == 0))
    def _():
        fetch(idx_ref, 0, tc)

    @pl.when(f == 0)
    def _():
        pltpu.make_async_copy(h_ref.at[pl.ds(0, tc)], rows, sem).wait()
        hi, lo = _unpack_bf16_pairs(rows[...])
        x_scr[:, :half] = hi.astype(BF16)
        x_scr[:, half:] = lo.astype(BF16)
        acc_ref[...] = jnp.zeros_like(acc_ref)

    @pl.when((f < GATHER_STEPS) & (tile + 1 < n_tiles))
    def _():
        fetch(nxt_ref, f * per_step, per_step)

    x = x_scr[...]
    a = jnp.dot(x, wg_ref[...].astype(BF16), preferred_element_type=F32)
    b = jnp.dot(x, wu_ref[...].astype(BF16), preferred_element_type=F32)
    hmid = (a * jax.nn.sigmoid(a) * b).astype(BF16)
    acc_ref[...] += jnp.dot(hmid, wd_ref[...].astype(BF16), preferred_element_type=F32)

    @pl.when(f == pl.num_programs(2) - 1)
    def _():
        y_ref[...] = _pack_bf16_pairs(acc_ref[...].astype(BF16).astype(F32))


def _experts(idx, h2p, w_gate, w_up, w_down, cap):
    n_exp, d, ff = w_gate.shape
    half = h2p.shape[1]
    tc = idx.shape[-1]
    tf = EXPERT_FF_TILE
    ct = cap // tc
    n_tiles = n_exp * ct
    assert cap % tc == 0 and ff % tf == 0 and ff // tf >= GATHER_STEPS
    assert tc % (GATHER_STEPS * GATHER_UNROLL) == 0
    smem = lambda im: pl.BlockSpec((None, 1, tc), im, memory_space=pltpu.SMEM)
    return pl.pallas_call(
        _expert_body,
        grid=(n_exp, ct, ff // tf),
        in_specs=[
            smem(lambda e, c, f: (e * ct + c, 0, 0)),
            smem(lambda e, c, f: (jnp.minimum(e * ct + c + 1, n_tiles - 1), 0, 0)),
            pl.BlockSpec(memory_space=pl.ANY),
            pl.BlockSpec((None, d, tf), lambda e, c, f: (e, 0, f)),
            pl.BlockSpec((None, d, tf), lambda e, c, f: (e, 0, f)),
            pl.BlockSpec((None, tf, d), lambda e, c, f: (e, f, 0)),
        ],
        out_specs=pl.BlockSpec((tc, half), lambda e, c, f: (e * ct + c, 0)),
        out_shape=jax.ShapeDtypeStruct((n_exp * cap, half), U32),
        scratch_shapes=[pltpu.VMEM((tc, half), U32), pltpu.VMEM((tc, d), BF16), pltpu.VMEM((tc, d), F32),
                        pltpu.SemaphoreType.DMA(())],
        compiler_params=pltpu.CompilerParams(
            dimension_semantics=("arbitrary", "arbitrary", "arbitrary"), vmem_limit_bytes=VMEM_LIMIT),
        name="experts",
    )(idx, idx, h2p, w_gate, w_up, w_down)


def _combine_body(cnt_ref, src_ref, x2_ref, wt_ref, y_ref, o_ref, g_scr, sem):
    n_rank, tb, half = g_scr.shape

    @pl.when(pl.program_id(0) == 0)
    def _():
        g_scr[...] = jnp.zeros_like(g_scr)

    def token(t, carry):
        issued, deepest = carry
        n = cnt_ref[0, t]

        def rank(r, c):
            pltpu.make_async_copy(y_ref.at[pl.ds(src_ref[0, t * n_rank + r], 1)],
                                  g_scr.at[r, pl.ds(t, 1)], sem).start()
            return c

        lax.fori_loop(0, n, rank, 0)
        return issued + n, jnp.maximum(deepest, n)

    issued, deepest = lax.fori_loop(0, tb, token, (0, 0))

    def drain(j, carry):
        pltpu.make_async_copy(y_ref.at[pl.ds(0, 1)], g_scr.at[0, pl.ds(0, 1)], sem).wait()
        return carry

    lax.fori_loop(0, issued, drain, 0)

    o_ref[...] = x2_ref[...]
    for r in range(n_rank):
        @pl.when(r < deepest)
        def _(r=r):
            w = wt_ref[:, r:r + 1]
            hi, lo = _unpack_bf16_pairs(g_scr[r])
            live = w > 0.0
            o_ref[:, :half] += jnp.where(live, w * hi, 0.0)
            o_ref[:, half:] += jnp.where(live, w * lo, 0.0)


def _combine(cntt, src, x2, wt, y):
    n, d = x2.shape
    tb = cntt.shape[-1]
    smem = lambda a: pl.BlockSpec((None, 1, a.shape[-1]), lambda i: (i, 0, 0), memory_space=pltpu.SMEM)
    return pl.pallas_call(
        _combine_body,
        grid=(n // tb,),
        in_specs=[
            smem(cntt), smem(src),
            pl.BlockSpec((tb, d), lambda i: (i, 0)),
            pl.BlockSpec((tb, N_EXPERTS), lambda i: (i, 0)),
            pl.BlockSpec(memory_space=pl.ANY),
        ],
        out_specs=pl.BlockSpec((tb, d), lambda i: (i, 0)),
        out_shape=jax.ShapeDtypeStruct((n, d), F32),
        scratch_shapes=[pltpu.VMEM((N_EXPERTS, tb, d // 2), U32), pltpu.SemaphoreType.DMA(())],
        compiler_params=pltpu.CompilerParams(
            dimension_semantics=("arbitrary",), vmem_limit_bytes=VMEM_LIMIT),
        name="combine",
    )(cntt, src, x2, wt, y)


def _layer(x, p):
    b, t, d = x.shape
    n = b * t
    head_dim = p["q_norm"].shape[-1]
    attn = N_HEADS * head_dim
    q, k, v, u = _in_proj(x.reshape(n, d), p["norm_mix"], p["w_in"], p["q_norm"], p["k_norm"],
                          attn=attn, head_dim=head_dim)
    shp = lambda a: a.reshape(b, t, a.shape[-1])
    x2, h2p, lg = _mix(shp(q), shp(k), shp(v), shp(u), x, p["bias_tabs"], p["w_pool"], p["pool_scale"],
                       p["w_out"], p["norm_ffn"], p["wr"], p["wr0"], head_dim=head_dim)

    nblk = n // LANES
    cap = max(1, (CAPACITY_FACTOR * n) // N_EXPERTS)
    lg = lg.reshape(n, LANES)[:, :N_EXPERTS]
    lg = lg.T.reshape(N_EXPERTS, nblk, LANES).transpose(0, 2, 1)
    idx, src, wt, cntt = _route(lg, cap)

    tc = min(EXPERT_TOKENS, cap)
    y = _experts(idx.reshape(N_EXPERTS * (cap // tc), 1, tc), h2p.reshape(n, d // 2),
                 p["w_gate"], p["w_up"], p["w_down"], cap)

    tb = min(COMBINE_TOKENS, n)
    src = src.transpose(2, 1, 0).reshape(n // tb, 1, tb * N_EXPERTS)
    wt = wt.transpose(2, 1, 0).reshape(n, N_EXPERTS)
    cntt = cntt.T.reshape(n // tb, 1, tb)
    out = _combine(cntt, src, x2.reshape(n, d), wt, y)
    return out.reshape(b, t, d)


def _prepare(l, norm_mix, w_in, q_norm, k_norm, rel_pos_bias, w_pool, pool_scale, w_out, norm_ffn,
             w_router, w_gate, w_up, w_down):
    d = w_in.shape[1]
    whi = w_router[l].astype(BF16)
    wlo = (w_router[l] - whi.astype(F32)).astype(BF16)
    pad = lambda *cols: jnp.concatenate(
        list(cols) + [jnp.zeros((d, LANES - sum(c.shape[1] for c in cols)), BF16)], axis=1)
    return dict(
        norm_mix=norm_mix[l][None, :], w_in=w_in[l].astype(BF16),
        q_norm=q_norm[l][None, :], k_norm=k_norm[l][None, :],
        bias_tabs=_bias_tables(rel_pos_bias[l]), w_pool=w_pool[l].astype(BF16),
        pool_scale=pool_scale[l][None, :], w_out=w_out[l].astype(BF16), norm_ffn=norm_ffn[l][None, :],
        wr=pad(whi, wlo), wr0=pad(whi),
        w_gate=w_gate[l], w_up=w_up[l], w_down=w_down[l],
    )


def kernel(x_prompt, x_sample, norm_mix, w_in, q_norm, k_norm, rel_pos_bias, w_pool, pool_scale, w_out,
           norm_ffn, w_router, w_gate, w_up, w_down):
    weights = (norm_mix, w_in, q_norm, k_norm, rel_pos_bias, w_pool, pool_scale, w_out, norm_ffn,
               w_router, w_gate, w_up, w_down)
    layers = [_prepare(l, *weights) for l in range(w_in.shape[0])]

    def trunk(x):
        for p in layers:
            x = _layer(x, p)
        return x

    return trunk(x_prompt), trunk(x_sample)
```

```python
import functools

import numpy as np
import jax
import jax.numpy as jnp
from jax import lax
from jax.experimental import pallas as pl
from jax.experimental.pallas import tpu as pltpu

F32 = jnp.float32
BF16 = jnp.bfloat16
I32 = jnp.int32
U32 = jnp.uint32

GRID_W = 64
WIN_R = 8
WIN_C = 16
N_HEADS = 8
POOL_WINDOWS = (2, 4, 8, 16)
N_EXPERTS = 16
CAPACITY_FACTOR = 2
EPS = 1e-6

LANES = 128
SUBLANES = 8
MASKED = -1e30
VMEM_LIMIT = 56 * 1024 * 1024
HI16 = 0xFFFF0000
DIGIT = 64.0

ROWS_PER_STEP = 4
POOL_HALO = 8
IN_PROJ_TOKENS = 512
EXPERT_TOKENS = 1024
EXPERT_FF_TILE = 256
GATHER_STEPS = 8
GATHER_UNROLL = 8
COMBINE_TOKENS = 128
COMBINE_ROWS = 128


def _pack_bf16_pairs(x):
    b = lax.bitcast_convert_type(x, U32)
    h = x.shape[1] // 2
    return b[:, :h] | (b[:, h:] >> 16)


def _unpack_bf16_pairs(w):
    hi = lax.bitcast_convert_type(w & jnp.uint32(HI16), F32)
    lo = lax.bitcast_convert_type(w << 16, F32)
    return hi, lo


def _in_proj_body(x_ref, g_ref, w_ref, qn_ref, kn_ref, q_ref, k_ref, v_ref, u_ref, *, attn, head_dim):
    x = x_ref[...]
    h = x * lax.rsqrt(jnp.mean(x * x, axis=-1, keepdims=True) + EPS) * g_ref[...]
    h = h.astype(BF16)

    def head_norm(z, gain, scale, out_ref):
        for hd in range(attn // head_dim):
            sl = slice(hd * head_dim, (hd + 1) * head_dim)
            zh = z[:, sl]
            r = lax.rsqrt(jnp.mean(zh * zh, axis=-1, keepdims=True) + EPS)
            out_ref[:, sl] = (zh * r * gain * scale).astype(BF16)

    zq = jnp.dot(h, w_ref[:, 0:attn], preferred_element_type=F32)
    head_norm(zq, qn_ref[...], head_dim ** -0.5, q_ref)
    zk = jnp.dot(h, w_ref[:, attn:2 * attn], preferred_element_type=F32)
    head_norm(zk, kn_ref[...], 1.0, k_ref)
    v_ref[...] = jnp.dot(h, w_ref[:, 2 * attn:3 * attn], preferred_element_type=F32).astype(BF16)
    u_ref[...] = jnp.dot(h, w_ref[:, 3 * attn:], preferred_element_type=F32)


def _in_proj(x, gain, w_in, q_norm, k_norm, *, attn, head_dim):
    n, d = x.shape
    width = w_in.shape[1]
    pool = width - 3 * attn
    tm = min(IN_PROJ_TOKENS, n)
    const = lambda i: (0, 0)
    tok = lambda i: (i, 0)
    return pl.pallas_call(
        functools.partial(_in_proj_body, attn=attn, head_dim=head_dim),
        grid=(n // tm,),
        in_specs=[
            pl.BlockSpec((tm, d), tok),
            pl.BlockSpec((1, d), const),
            pl.BlockSpec((d, width), const, pipeline_mode=pl.Buffered(1)),
            pl.BlockSpec((1, head_dim), const),
            pl.BlockSpec((1, head_dim), const),
        ],
        out_specs=[
            pl.BlockSpec((tm, attn), tok),
            pl.BlockSpec((tm, attn), tok),
            pl.BlockSpec((tm, attn), tok),
            pl.BlockSpec((tm, pool), tok),
        ],
        out_shape=[
            jax.ShapeDtypeStruct((n, attn), BF16),
            jax.ShapeDtypeStruct((n, attn), BF16),
            jax.ShapeDtypeStruct((n, attn), BF16),
            jax.ShapeDtypeStruct((n, pool), F32),
        ],
        compiler_params=pltpu.CompilerParams(
            dimension_semantics=("parallel",), vmem_limit_bytes=VMEM_LIMIT),
        name="in_proj",
    )(x, gain, w_in, q_norm, k_norm)


def _bias_tables(rpb):
    rq = ROWS_PER_STEP
    nk = 3 * rq
    qi = np.arange(rq)[:, None]
    kj = np.arange(nk)[None, :]
    d = kj - rq - qi
    dr = np.clip(d + WIN_R - 1, 0, 2 * WIN_R - 2)
    ones = np.ones((rq, nk), bool)
    valid_first = ones & (kj >= rq) & (kj - rq < WIN_R)
    valid_mid = (d >= -(WIN_R // 2)) & (d < WIN_R // 2)
    valid_last = ones & (kj < 2 * rq)
    c = np.arange(GRID_W)[:, None]
    kc = np.arange(GRID_W)[None, :]
    cs = np.clip(c - WIN_C // 2, 0, GRID_W - WIN_C)
    valid_c = (kc >= cs) & (kc < cs + WIN_C)
    padded = jnp.pad(rpb, ((0, 0), (0, 0), (GRID_W, GRID_W)))
    first = GRID_W + WIN_C - 1
    by_col = jnp.stack([padded[:, :, first - ci:first - ci + GRID_W] for ci in range(GRID_W)],
                       axis=2)
    vals = jnp.stack([jnp.stack([by_col[:, dr[a, b]] for b in range(nk)], axis=2) for a in range(rq)],
                     axis=1)
    tabs = []
    for vr in (valid_first, valid_mid, valid_last):
        ok = vr[:, None, :, None] & valid_c[None, :, None, :]
        t = jnp.where(ok[None], vals, MASKED)
        tabs.append(t.reshape(rpb.shape[0], rq * GRID_W, nk * GRID_W))
    return jnp.stack(tabs).astype(F32)


def _mix_body(q_ref, kp_ref, kc_ref, kn_ref, vp_ref, vc_ref, vn_ref, bt_ref, uc_ref, up_ref, un_ref, x_ref,
              wpool_ref, pscale_ref, wout_ref, nffn_ref, wr_ref, wr0_ref,
              x2_ref, h2p_ref, lg_ref, mix_ref, *, seq_len, head_dim, n_heads, pool_ch):
    i = pl.program_id(1)
    nb = pl.num_programs(1)
    tq = q_ref.shape[0]
    attn = n_heads * head_dim

    kcat = jnp.concatenate([kp_ref[...], kc_ref[...], kn_ref[...]], axis=0)
    vcat = jnp.concatenate([vp_ref[...], vc_ref[...], vn_ref[...]], axis=0)
    for h in range(n_heads):
        sl = slice(h * head_dim, (h + 1) * head_dim)
        s = lax.dot_general(q_ref[:, sl], kcat[:, sl], (((1,), (1,)), ((), ())),
                            preferred_element_type=F32)
        s = s + bt_ref[h]
        m = jnp.max(s, axis=-1, keepdims=True)
        p = jnp.exp(s - m)
        l = jnp.sum(p, axis=-1, keepdims=True)
        o = jnp.dot(p.astype(BF16), vcat[:, sl], preferred_element_type=F32)
        mix_ref[:, sl] = (o / l).astype(BF16)

    ucur = uc_ref[...]
    up = jnp.where(i == 0, 0.0, up_ref[...])
    un = jnp.where(i == nb - 1, 0.0, un_ref[...])
    uext = jnp.concatenate([up, ucur, un], axis=0)
    t = i * tq + lax.broadcasted_iota(I32, (tq, 1), 0)
    for g, w in enumerate(POOL_WINDOWS):
        gs = slice(g * pool_ch, (g + 1) * pool_ch)
        ug = uext[:, gs]
        arr = ug[:-1] + ug[1:]
        sh = 2
        while sh < w:
            arr = arr[:-sh] + arr[sh:]
            sh *= 2
        start = POOL_HALO - w // 2
        wsum = arr[start:start + tq]
        cnt = (jnp.minimum(t + w // 2, seq_len) - jnp.maximum(t - w // 2, 0)).astype(F32)
        pg = (wsum / cnt - ucur[:, gs]).astype(BF16)
        yg = jnp.dot(pg, wpool_ref[g], preferred_element_type=F32) * pscale_ref[:, gs]
        mix_ref[:, attn + g * pool_ch:attn + (g + 1) * pool_ch] = yg.astype(BF16)

    x2 = x_ref[...] + jnp.dot(mix_ref[...], wout_ref[...], preferred_element_type=F32)
    x2_ref[...] = x2
    hn = x2 * lax.rsqrt(jnp.mean(x2 * x2, axis=-1, keepdims=True) + EPS) * nffn_ref[...]
    hi = hn.astype(BF16)
    hi32 = hi.astype(F32)
    lo = (hn - hi32).astype(BF16)
    r = (jnp.dot(hi, wr_ref[...], preferred_element_type=F32)
         + jnp.dot(lo, wr0_ref[...], preferred_element_type=F32))
    lg_ref[...] = r + pltpu.roll(r, LANES - N_EXPERTS, 1)
    h2p_ref[...] = _pack_bf16_pairs(hi32)


def _mix(q, k, v, u, x, bias_tabs, w_pool, pool_scale, w_out, norm_ffn, wr, wr0, *, head_dim):
    b, t, attn = q.shape
    d = x.shape[-1]
    pool = u.shape[-1]
    n_heads = attn // head_dim
    pool_ch = pool // len(POOL_WINDOWS)
    tq = ROWS_PER_STEP * GRID_W
    nb = t // tq
    assert 2 * ROWS_PER_STEP == WIN_R and t % tq == 0 and nb >= 2
    hb = tq // POOL_HALO

    cur = lambda bi, i: (bi, i, 0)
    prev = lambda bi, i: (bi, jnp.maximum(i - 1, 0), 0)
    nxt = lambda bi, i: (bi, jnp.minimum(i + 1, nb - 1), 0)
    c2 = lambda bi, i: (0, 0)
    c3 = lambda bi, i: (0, 0, 0)
    kv = lambda im: pl.BlockSpec((None, tq, attn), im)
    return pl.pallas_call(
        functools.partial(_mix_body, seq_len=t, head_dim=head_dim, n_heads=n_heads, pool_ch=pool_ch),
        grid=(b, nb),
        in_specs=[
            kv(cur), kv(prev), kv(cur), kv(nxt), kv(prev), kv(cur), kv(nxt),
            pl.BlockSpec((None, n_heads, tq, 3 * tq),
                         lambda bi, i: (jnp.where(i == 0, 0, jnp.where(i == nb - 1, 2, 1)), 0, 0, 0)),
            pl.BlockSpec((None, tq, pool), cur),
            pl.BlockSpec((None, POOL_HALO, pool), lambda bi, i: (bi, jnp.maximum(i * hb - 1, 0), 0)),
            pl.BlockSpec((None, POOL_HALO, pool),
                         lambda bi, i: (bi, jnp.minimum((i + 1) * hb, t // POOL_HALO - 1), 0)),
            pl.BlockSpec((None, tq, d), cur),
            pl.BlockSpec(w_pool.shape, c3),
            pl.BlockSpec((1, pool), c2),
            pl.BlockSpec(w_out.shape, c2, pipeline_mode=pl.Buffered(1)),
            pl.BlockSpec((1, d), c2),
            pl.BlockSpec(wr.shape, c2),
            pl.BlockSpec(wr0.shape, c2),
        ],
        out_specs=[
            pl.BlockSpec((None, tq, d), cur),
            pl.BlockSpec((None, tq, d // 2), cur),
            pl.BlockSpec((None, tq, LANES), cur),
        ],
        out_shape=[
            jax.ShapeDtypeStruct((b, t, d), F32),
            jax.ShapeDtypeStruct((b, t, d // 2), U32),
            jax.ShapeDtypeStruct((b, t, LANES), F32),
        ],
        scratch_shapes=[pltpu.VMEM((tq, attn + pool), BF16)],
        compiler_params=pltpu.CompilerParams(
            dimension_semantics=("parallel", "arbitrary"), vmem_limit_bytes=VMEM_LIMIT),
        name="mix",
    )(q, k, k, k, v, v, v, bias_tabs, u, u, u, x, w_pool, pool_scale, w_out, norm_ffn, wr, wr0)


def _route_body(lg_ref, idx_ref, dst_ref, wt_ref, off_ref, aff_ref, sel_ref, rank_ref, thr_ref, *, cap):
    n_exp, tl, nblk = lg_ref.shape

    m = lg_ref[0]
    for e in range(1, n_exp):
        m = jnp.maximum(m, lg_ref[e])
    den = jnp.zeros_like(m)
    for e in range(n_exp):
        ex = jnp.exp(lg_ref[e] - m)
        aff_ref[e] = ex
        den = den + ex
    for e in range(n_exp):
        aff_ref[e] = aff_ref[e] / den

    def total(x):
        return jnp.sum(jnp.sum(x, axis=0, keepdims=True), axis=1, keepdims=True)

    def search(it, thr):
        bit = jnp.left_shift(jnp.int32(1), 30 - it)
        out = []
        for e in range(n_exp):
            bits = lax.bitcast_convert_type(aff_ref[e], I32)
            cand = thr[e] | bit
            c = total(jnp.where(bits >= cand, 1.0, 0.0))
            out.append(jnp.where(c >= cap, cand, thr[e]))
        return tuple(out)

    thr = lax.fori_loop(0, 31, search, tuple(jnp.zeros((1, 1), I32) for _ in range(n_exp)))
    for e in range(n_exp):
        thr_ref[e] = jnp.broadcast_to(thr[e], (SUBLANES, LANES))

    def tri():
        r_i = lax.broadcasted_iota(I32, (tl, tl), 0)
        c_i = lax.broadcasted_iota(I32, (tl, tl), 1)
        return jnp.where(c_i <= r_i, 1.0, 0.0).astype(BF16)

    def before(col_form=False):
        rb = lax.broadcasted_iota(I32, (nblk, nblk), 0)
        cb = lax.broadcasted_iota(I32, (nblk, nblk), 1)
        return jnp.where((cb < rb) if col_form else (rb < cb), 1.0, 0.0).astype(BF16)

    def prefix(mask):
        incl = jnp.dot(tri(), mask.astype(BF16), preferred_element_type=F32)
        tot8 = jnp.broadcast_to(incl[tl - 1:tl, :], (SUBLANES, nblk)).astype(BF16)
        base = jnp.dot(tot8, before(), preferred_element_type=F32)[0:1, :]
        return incl, base

    for r in range(n_exp):
        wt_ref[r] = jnp.zeros((tl, nblk), F32)

    def select(e, rank):
        aff = aff_ref[e]
        bits = lax.bitcast_convert_type(aff, I32)
        thr_e = thr_ref[e][0:1, 0:1]
        gt = bits > thr_e
        eq = bits == thr_e
        need = cap - total(jnp.where(gt, 1.0, 0.0))
        eqf = jnp.where(eq, 1.0, 0.0)
        incl_eq, base_eq = prefix(eqf)
        eq_rank = base_eq + incl_eq - eqf
        sel = gt | (eq & (eq_rank < need))
        self_ = jnp.where(sel, 1.0, 0.0)
        for r in range(n_exp):
            wt_ref[r] = jnp.where(sel & (rank == r), aff, wt_ref[r])
        sel_ref[e] = self_
        rank_ref[e] = rank
        return rank + self_

    rank = lax.fori_loop(0, n_exp, select, jnp.zeros((tl, nblk), F32))

    incl_n = jnp.dot(tri(), rank.astype(BF16), preferred_element_type=F32)
    tot_n = jnp.broadcast_to(incl_n[tl - 1:tl, :], (SUBLANES, nblk))
    tot_hi = jnp.floor(tot_n * (1.0 / DIGIT))
    tot_lo = tot_n - tot_hi * DIGIT
    base_n = (DIGIT * jnp.dot(tot_hi.astype(BF16), before(), preferred_element_type=F32)
              + jnp.dot(tot_lo.astype(BF16), before(), preferred_element_type=F32))[0:1, :]
    off_ref[...] = base_n + incl_n - rank

    def place(e, carry):
        slot_f = lax.broadcasted_iota(I32, (nblk, cap), 1).astype(F32)
        blk_f = lax.broadcasted_iota(I32, (nblk, cap), 0).astype(F32)
        tl_f = lax.broadcasted_iota(I32, (tl, cap), 0).astype(F32)
        self_ = sel_ref[e]
        incl, _ = prefix(self_)
        tot_c = jnp.sum(self_.T, axis=1, keepdims=True)
        base_c = jnp.dot(before(col_form=True), jnp.broadcast_to(tot_c, (nblk, LANES)).astype(BF16),
                         preferred_element_type=F32)[:, 0:1]
        owns = (base_c <= slot_f) & (slot_f < base_c + tot_c)
        local = jnp.sum(jnp.where(owns, slot_f - base_c, 0.0), axis=0, keepdims=True)
        blk = jnp.sum(jnp.where(owns, blk_f, 0.0), axis=0, keepdims=True)
        owns_b = jnp.where(owns, 1.0, 0.0).astype(BF16)
        incl_at = jnp.dot(incl.astype(BF16), owns_b, preferred_element_type=F32)
        pos = jnp.sum(jnp.where(incl_at <= local, 1.0, 0.0), axis=0, keepdims=True)
        idx_ref[e] = (blk * tl + pos).astype(I32)

        row = off_ref[...] + rank_ref[e]
        d2 = jnp.floor(row * (1.0 / (DIGIT * DIGIT)))
        rem = row - d2 * (DIGIT * DIGIT)
        d1 = jnp.floor(rem * (1.0 / DIGIT))
        d0 = rem - d1 * DIGIT
        at = lambda dg: jnp.dot(dg.astype(BF16), owns_b, preferred_element_type=F32)
        row_at = (DIGIT * DIGIT) * at(d2) + DIGIT * at(d1) + at(d0)
        dst_ref[e] = jnp.sum(jnp.where(tl_f == pos, row_at, 0.0), axis=0, keepdims=True).astype(I32)
        return carry

    lax.fori_loop(0, n_exp, place, 0)


def _route(lg, cap):
    n_exp, tl, nblk = lg.shape
    full = lambda: pl.BlockSpec(memory_space=pltpu.VMEM)
    return pl.pallas_call(
        functools.partial(_route_body, cap=cap),
        in_specs=[full()],
        out_specs=[full() for _ in range(4)],
        out_shape=[
            jax.ShapeDtypeStruct((n_exp, 1, cap), I32),
            jax.ShapeDtypeStruct((n_exp, 1, cap), I32),
            jax.ShapeDtypeStruct((n_exp, tl, nblk), F32),
            jax.ShapeDtypeStruct((tl, nblk), F32),
        ],
        scratch_shapes=[pltpu.VMEM((n_exp, tl, nblk), F32) for _ in range(3)]
                       + [pltpu.VMEM((n_exp, SUBLANES, LANES), I32)],
        compiler_params=pltpu.CompilerParams(vmem_limit_bytes=VMEM_LIMIT),
        name="route",
    )(lg)


def _expert_body(idx_ref, nxt_ref, dst_ref, h_ref, wg_ref, wu_ref, wd_ref, z_ref, rows, x_scr, acc_ref, y_scr,
                 sem, out_sem):
    e, c, f = pl.program_id(0), pl.program_id(1), pl.program_id(2)
    tile = e * pl.num_programs(1) + c
    n_tiles = pl.num_programs(0) * pl.num_programs(1)
    tc, half = rows.shape
    per_step = tc // GATHER_STEPS

    def fetch(tok_ref, first, count):
        def group(g, carry):
            for k in range(GATHER_UNROLL):
                r = first + g * GATHER_UNROLL + k
                pltpu.make_async_copy(h_ref.at[pl.ds(tok_ref[0, r], 1)], rows.at[pl.ds(r, 1)], sem).start()
            return carry
        lax.fori_loop(0, count // GATHER_UNROLL, group, 0)

    @pl.when((tile == 0) & (f == 0))
    def _():
        fetch(idx_ref, 0, tc)

    @pl.when(f == 0)
    def _():
        pltpu.make_async_copy(h_ref.at[pl.ds(0, tc)], rows, sem).wait()
        hi, lo = _unpack_bf16_pairs(rows[...])
        x_scr[:, :half] = hi.astype(BF16)
        x_scr[:, half:] = lo.astype(BF16)
        acc_ref[...] = jnp.zeros_like(acc_ref)

    @pl.when((f < GATHER_STEPS) & (tile + 1 < n_tiles))
    def _():
        fetch(nxt_ref, f * per_step, per_step)

    x = x_scr[...]
    a = jnp.dot(x, wg_ref[...].astype(BF16), preferred_element_type=F32)
    b = jnp.dot(x, wu_ref[...].astype(BF16), preferred_element_type=F32)
    hmid = (a * jax.nn.sigmoid(a) * b).astype(BF16)
    acc_ref[...] += jnp.dot(hmid, wd_ref[...].astype(BF16), preferred_element_type=F32)

    def all_rows_out():
        return pltpu.make_async_copy(y_scr, z_ref.at[pl.ds(0, tc)], out_sem)

    @pl.when(f == pl.num_programs(2) - 1)
    def _():
        @pl.when(tile > 0)
        def _():
            all_rows_out().wait()

        y_scr[...] = _pack_bf16_pairs(acc_ref[...].astype(BF16).astype(F32))

        def group(g, carry):
            for k in range(GATHER_UNROLL):
                r = g * GATHER_UNROLL + k
                pltpu.make_async_copy(y_scr.at[pl.ds(r, 1)], z_ref.at[pl.ds(dst_ref[0, r], 1)], out_sem).start()
            return carry
        lax.fori_loop(0, tc // GATHER_UNROLL, group, 0)

        @pl.when(tile == n_tiles - 1)
        def _():
            all_rows_out().wait()


def _experts(idx, dst, h2p, w_gate, w_up, w_down, cap):
    n_exp, d, ff = w_gate.shape
    half = h2p.shape[1]
    tc = idx.shape[-1]
    tf = EXPERT_FF_TILE
    ct = cap // tc
    n_tiles = n_exp * ct
    assert cap % tc == 0 and ff % tf == 0 and ff // tf >= GATHER_STEPS
    assert tc % (GATHER_STEPS * GATHER_UNROLL) == 0
    smem = lambda im: pl.BlockSpec((None, 1, tc), im, memory_space=pltpu.SMEM)
    return pl.pallas_call(
        _expert_body,
        grid=(n_exp, ct, ff // tf),
        in_specs=[
            smem(lambda e, c, f: (e * ct + c, 0, 0)),
            smem(lambda e, c, f: (jnp.minimum(e * ct + c + 1, n_tiles - 1), 0, 0)),
            smem(lambda e, c, f: (e * ct + c, 0, 0)),
            pl.BlockSpec(memory_space=pl.ANY),
            pl.BlockSpec((None, d, tf), lambda e, c, f: (e, 0, f)),
            pl.BlockSpec((None, d, tf), lambda e, c, f: (e, 0, f)),
            pl.BlockSpec((None, tf, d), lambda e, c, f: (e, f, 0)),
        ],
        out_specs=pl.BlockSpec(memory_space=pl.ANY),
        out_shape=jax.ShapeDtypeStruct((n_exp * cap, half), U32),
        scratch_shapes=[pltpu.VMEM((tc, half), U32), pltpu.VMEM((tc, d), BF16), pltpu.VMEM((tc, d), F32),
                        pltpu.VMEM((tc, half), U32), pltpu.SemaphoreType.DMA(()), pltpu.SemaphoreType.DMA(())],
        compiler_params=pltpu.CompilerParams(
            dimension_semantics=("arbitrary", "arbitrary", "arbitrary"), vmem_limit_bytes=VMEM_LIMIT,
            has_side_effects=True),
        name="experts",
    )(idx, idx, dst, h2p, w_gate, w_up, w_down)


def _combine_body(start_ref, x2_ref, off_ref, wt_ref, z_ref, o_ref, buf, gate_scr, sem):
    i = pl.program_id(0)
    nsteps = pl.num_programs(0)
    _, _, rc, half = buf.shape
    n_rows = z_ref.shape[0]

    def first_row(b):
        return (start_ref[b] // SUBLANES) * SUBLANES

    def n_chunks(b):
        return (start_ref[b + 1] - first_row(b) + rc - 1) // rc

    def chunk_copy(b, k, slot):
        row = pl.multiple_of(jnp.minimum(first_row(b) + k * rc, n_rows - rc), SUBLANES)
        return pltpu.make_async_copy(z_ref.at[pl.ds(row, rc)], buf.at[slot, k], sem.at[slot])

    def fetch(b, slot):
        def one(k, carry):
            chunk_copy(b, k, slot).start()
            return carry
        lax.fori_loop(0, n_chunks(b), one, 0)

    @pl.when(i == 0)
    def _():
        fetch(0, 0)

    @pl.when(i + 1 < nsteps)
    def _():
        fetch(i + 1, (i + 1) % 2)

    slot = i % 2
    o_ref[...] = x2_ref[...]
    tb = o_ref.shape[0]
    off = jnp.broadcast_to(off_ref[...], (tb, rc))
    for r in range(wt_ref.shape[1]):
        gate_scr[r] = jnp.broadcast_to(wt_ref[:, r:r + 1], (tb, rc))
    lane = lax.broadcasted_iota(I32, (1, rc), 1)

    def landed(k, carry):
        chunk_copy(i, k, slot).wait()
        return carry

    lax.fori_loop(0, n_chunks(i), landed, 0)

    def chunk(k, carry):
        want = first_row(i) + k * rc
        row = jnp.minimum(want, n_rows - rc) + lane
        rel = jnp.where(row >= want, row.astype(F32) - off, -1.0)
        a = jnp.zeros(rel.shape, F32)
        for r in range(wt_ref.shape[1]):
            a = jnp.where(rel == r, gate_scr[r], a)
        a_hi = a.astype(BF16)
        a_lo = (a - a_hi.astype(F32)).astype(BF16)
        z_hi, z_lo = _unpack_bf16_pairs(buf[slot, k])
        z_hi, z_lo = z_hi.astype(BF16), z_lo.astype(BF16)
        seg = lambda zz: (jnp.dot(a_hi, zz, preferred_element_type=F32)
                          + jnp.dot(a_lo, zz, preferred_element_type=F32))
        o_ref[:, :half] += seg(z_hi)
        o_ref[:, half:] += seg(z_lo)
        return carry

    lax.fori_loop(0, n_chunks(i), chunk, 0)


def _combine(starts, x2, off, wt, z):
    n, d = x2.shape
    tb = COMBINE_TOKENS
    rc = COMBINE_ROWS
    max_chunks = (N_EXPERTS * tb + SUBLANES - 1 + rc - 1) // rc + 1
    assert n % tb == 0 and z.shape[0] % SUBLANES == 0 and z.shape[0] >= rc
    return pl.pallas_call(
        _combine_body,
        grid_spec=pltpu.PrefetchScalarGridSpec(
            num_scalar_prefetch=1,
            grid=(n // tb,),
            in_specs=[
                pl.BlockSpec((tb, d), lambda i, s: (i, 0)),
                pl.BlockSpec((tb, 1), lambda i, s: (i, 0)),
                pl.BlockSpec((tb, N_EXPERTS), lambda i, s: (i, 0)),
                pl.BlockSpec(memory_space=pl.ANY),
            ],
            out_specs=pl.BlockSpec((tb, d), lambda i, s: (i, 0)),
            scratch_shapes=[pltpu.VMEM((2, max_chunks, rc, d // 2), U32),
                            pltpu.VMEM((N_EXPERTS, tb, rc), F32), pltpu.SemaphoreType.DMA((2,))],
        ),
        out_shape=jax.ShapeDtypeStruct((n, d), F32),
        compiler_params=pltpu.CompilerParams(
            dimension_semantics=("arbitrary",), vmem_limit_bytes=VMEM_LIMIT),
        name="combine",
    )(starts, x2, off, wt, z)


def _layer(x, p):
    b, t, d = x.shape
    n = b * t
    head_dim = p["q_norm"].shape[-1]
    attn = N_HEADS * head_dim
    q, k, v, u = _in_proj(x.reshape(n, d), p["norm_mix"], p["w_in"], p["q_norm"], p["k_norm"],
                          attn=attn, head_dim=head_dim)
    shp = lambda a: a.reshape(b, t, a.shape[-1])
    x2, h2p, lg = _mix(shp(q), shp(k), shp(v), shp(u), x, p["bias_tabs"], p["w_pool"], p["pool_scale"],
                       p["w_out"], p["norm_ffn"], p["wr"], p["wr0"], head_dim=head_dim)

    nblk = n // LANES
    cap = max(1, (CAPACITY_FACTOR * n) // N_EXPERTS)
    lg = lg.reshape(n, LANES)[:, :N_EXPERTS]
    lg = lg.T.reshape(N_EXPERTS, nblk, LANES).transpose(0, 2, 1)
    idx, dst, wt, off = _route(lg, cap)

    tc = min(EXPERT_TOKENS, cap)
    tiles = lambda a: a.reshape(N_EXPERTS * (cap // tc), 1, tc)
    z = _experts(tiles(idx), tiles(dst), h2p.reshape(n, d // 2), p["w_gate"], p["w_up"], p["w_down"], cap)

    assert COMBINE_TOKENS == LANES
    starts = jnp.concatenate([off[0, :].astype(I32), jnp.full((1,), N_EXPERTS * cap, I32)])
    wt = wt.transpose(2, 1, 0).reshape(n, N_EXPERTS)
    out = _combine(starts, x2.reshape(n, d), off.T.reshape(n, 1), wt, z)
    return out.reshape(b, t, d)


def _prepare(l, norm_mix, w_in, q_norm, k_norm, rel_pos_bias, w_pool, pool_scale, w_out, norm_ffn,
             w_router, w_gate, w_up, w_down):
    d = w_in.shape[1]
    whi = w_router[l].astype(BF16)
    wlo = (w_router[l] - whi.astype(F32)).astype(BF16)
    pad = lambda *cols: jnp.concatenate(
        list(cols) + [jnp.zeros((d, LANES - sum(c.shape[1] for c in cols)), BF16)], axis=1)
    return dict(
        norm_mix=norm_mix[l][None, :], w_in=w_in[l].astype(BF16),
        q_norm=q_norm[l][None, :], k_norm=k_norm[l][None, :],
        bias_tabs=_bias_tables(rel_pos_bias[l]), w_pool=w_pool[l].astype(BF16),
        pool_scale=pool_scale[l][None, :], w_out=w_out[l].astype(BF16), norm_ffn=norm_ffn[l][None, :],
        wr=pad(whi, wlo), wr0=pad(whi),
        w_gate=w_gate[l], w_up=w_up[l], w_down=w_down[l],
    )


def kernel(x_prompt, x_sample, norm_mix, w_in, q_norm, k_norm, rel_pos_bias, w_pool, pool_scale, w_out,
           norm_ffn, w_router, w_gate, w_up, w_down):
    weights = (norm_mix, w_in, q_norm, k_norm, rel_pos_bias, w_pool, pool_scale, w_out, norm_ffn,
               w_router, w_gate, w_up, w_down)
    layers = [_prepare(l, *weights) for l in range(w_in.shape[0])]

    def trunk(x):
        for p in layers:
            x = _layer(x, p)
        return x

    return trunk(x_prompt), trunk(x_sample)
```

```python
import functools

import numpy as np
import jax
import jax.numpy as jnp
from jax import lax
from jax.experimental import pallas as pl
from jax.experimental.pallas import tpu as pltpu

F32 = jnp.float32
BF16 = jnp.bfloat16
I32 = jnp.int32
U32 = jnp.uint32

GRID_W = 64
WIN_R = 8
WIN_C = 16
N_HEADS = 8
POOL_WINDOWS = (2, 4, 8, 16)
N_EXPERTS = 16
CAPACITY_FACTOR = 2
EPS = 1e-6

LANES = 128
SUBLANES = 8
MASKED = -1e30
VMEM_LIMIT = 56 * 1024 * 1024
HI16 = 0xFFFF0000
DIGIT = 64.0

ROWS_PER_STEP = 4
POOL_HALO = 8
IN_PROJ_TOKENS = 512
EXPERT_TOKENS = 1024
EXPERT_FF_STEPS = 11
ROW_UNROLL = 8
COMBINE_TOKENS = 128
COMBINE_ROWS = 128


def _pack_bf16_pairs(x):
    b = lax.bitcast_convert_type(x, U32)
    h = x.shape[1] // 2
    return b[:, :h] | (b[:, h:] >> 16)


def _unpack_bf16_pairs(w):
    hi = lax.bitcast_convert_type(w & jnp.uint32(HI16), F32)
    lo = lax.bitcast_convert_type(w << 16, F32)
    return hi, lo


def _in_proj_body(x_ref, g_ref, w_ref, qn_ref, kn_ref, q_ref, k_ref, v_ref, u_ref, *, attn, head_dim):
    x = x_ref[...]
    h = x * lax.rsqrt(jnp.mean(x * x, axis=-1, keepdims=True) + EPS) * g_ref[...]
    h = h.astype(BF16)

    def head_norm(z, gain, scale, out_ref):
        for hd in range(attn // head_dim):
            sl = slice(hd * head_dim, (hd + 1) * head_dim)
            zh = z[:, sl]
            r = lax.rsqrt(jnp.mean(zh * zh, axis=-1, keepdims=True) + EPS)
            out_ref[:, sl] = (zh * r * gain * scale).astype(BF16)

    zq = jnp.dot(h, w_ref[:, 0:attn], preferred_element_type=F32)
    head_norm(zq, qn_ref[...], head_dim ** -0.5, q_ref)
    zk = jnp.dot(h, w_ref[:, attn:2 * attn], preferred_element_type=F32)
    head_norm(zk, kn_ref[...], 1.0, k_ref)
    v_ref[...] = jnp.dot(h, w_ref[:, 2 * attn:3 * attn], preferred_element_type=F32).astype(BF16)
    u_ref[...] = jnp.dot(h, w_ref[:, 3 * attn:], preferred_element_type=F32)


def _in_proj(x, gain, w_in, q_norm, k_norm, *, attn, head_dim):
    n, d = x.shape
    width = w_in.shape[1]
    pool = width - 3 * attn
    tm = min(IN_PROJ_TOKENS, n)
    const = lambda i: (0, 0)
    tok = lambda i: (i, 0)
    return pl.pallas_call(
        functools.partial(_in_proj_body, attn=attn, head_dim=head_dim),
        grid=(n // tm,),
        in_specs=[
            pl.BlockSpec((tm, d), tok),
            pl.BlockSpec((1, d), const),
            pl.BlockSpec((d, width), const, pipeline_mode=pl.Buffered(1)),
            pl.BlockSpec((1, head_dim), const),
            pl.BlockSpec((1, head_dim), const),
        ],
        out_specs=[
            pl.BlockSpec((tm, attn), tok),
            pl.BlockSpec((tm, attn), tok),
            pl.BlockSpec((tm, attn), tok),
            pl.BlockSpec((tm, pool), tok),
        ],
        out_shape=[
            jax.ShapeDtypeStruct((n, attn), BF16),
            jax.ShapeDtypeStruct((n, attn), BF16),
            jax.ShapeDtypeStruct((n, attn), BF16),
            jax.ShapeDtypeStruct((n, pool), F32),
        ],
        compiler_params=pltpu.CompilerParams(
            dimension_semantics=("parallel",), vmem_limit_bytes=VMEM_LIMIT),
        name="in_proj",
    )(x, gain, w_in, q_norm, k_norm)


def _bias_tables(rpb):
    rq = ROWS_PER_STEP
    nk = 3 * rq
    qi = np.arange(rq)[:, None]
    kj = np.arange(nk)[None, :]
    d = kj - rq - qi
    dr = np.clip(d + WIN_R - 1, 0, 2 * WIN_R - 2)
    ones = np.ones((rq, nk), bool)
    valid_first = ones & (kj >= rq) & (kj - rq < WIN_R)
    valid_mid = (d >= -(WIN_R // 2)) & (d < WIN_R // 2)
    valid_last = ones & (kj < 2 * rq)
    c = np.arange(GRID_W)[:, None]
    kc = np.arange(GRID_W)[None, :]
    cs = np.clip(c - WIN_C // 2, 0, GRID_W - WIN_C)
    valid_c = (kc >= cs) & (kc < cs + WIN_C)
    padded = jnp.pad(rpb, ((0, 0), (0, 0), (GRID_W, GRID_W)))
    first = GRID_W + WIN_C - 1
    by_col = jnp.stack([padded[:, :, first - ci:first - ci + GRID_W] for ci in range(GRID_W)],
                       axis=2)
    vals = jnp.stack([jnp.stack([by_col[:, dr[a, b]] for b in range(nk)], axis=2) for a in range(rq)],
                     axis=1)
    tabs = []
    for vr in (valid_first, valid_mid, valid_last):
        ok = vr[:, None, :, None] & valid_c[None, :, None, :]
        t = jnp.where(ok[None], vals, MASKED)
        tabs.append(t.reshape(rpb.shape[0], rq * GRID_W, nk * GRID_W))
    return jnp.stack(tabs).astype(F32)


def _mix_body(q_ref, kp_ref, kc_ref, kn_ref, vp_ref, vc_ref, vn_ref, bt_ref, uc_ref, up_ref, un_ref, x_ref,
              wpool_ref, pscale_ref, wout_ref, nffn_ref, wr_ref, wr0_ref,
              x2_ref, h2p_ref, lg_ref, mix_ref, *, seq_len, head_dim, n_heads, pool_ch):
    i = pl.program_id(1)
    nb = pl.num_programs(1)
    tq = q_ref.shape[0]
    attn = n_heads * head_dim

    kcat = jnp.concatenate([kp_ref[...], kc_ref[...], kn_ref[...]], axis=0)
    vcat = jnp.concatenate([vp_ref[...], vc_ref[...], vn_ref[...]], axis=0)
    for h in range(n_heads):
        sl = slice(h * head_dim, (h + 1) * head_dim)
        s = lax.dot_general(q_ref[:, sl], kcat[:, sl], (((1,), (1,)), ((), ())),
                            preferred_element_type=F32)
        s = s + bt_ref[h]
        m = jnp.max(s, axis=-1, keepdims=True)
        p = jnp.exp(s - m)
        l = jnp.sum(p, axis=-1, keepdims=True)
        o = jnp.dot(p.astype(BF16), vcat[:, sl], preferred_element_type=F32)
        mix_ref[:, sl] = (o / l).astype(BF16)

    ucur = uc_ref[...]
    up = jnp.where(i == 0, 0.0, up_ref[...])
    un = jnp.where(i == nb - 1, 0.0, un_ref[...])
    uext = jnp.concatenate([up, ucur, un], axis=0)
    t = i * tq + lax.broadcasted_iota(I32, (tq, 1), 0)
    for g, w in enumerate(POOL_WINDOWS):
        gs = slice(g * pool_ch, (g + 1) * pool_ch)
        ug = uext[:, gs]
        arr = ug[:-1] + ug[1:]
        sh = 2
        while sh < w:
            arr = arr[:-sh] + arr[sh:]
            sh *= 2
        start = POOL_HALO - w // 2
        wsum = arr[start:start + tq]
        cnt = (jnp.minimum(t + w // 2, seq_len) - jnp.maximum(t - w // 2, 0)).astype(F32)
        pg = (wsum / cnt - ucur[:, gs]).astype(BF16)
        yg = jnp.dot(pg, wpool_ref[g], preferred_element_type=F32) * pscale_ref[:, gs]
        mix_ref[:, attn + g * pool_ch:attn + (g + 1) * pool_ch] = yg.astype(BF16)

    x2 = x_ref[...] + jnp.dot(mix_ref[...], wout_ref[...], preferred_element_type=F32)
    x2_ref[...] = x2
    hn = x2 * lax.rsqrt(jnp.mean(x2 * x2, axis=-1, keepdims=True) + EPS) * nffn_ref[...]
    hi = hn.astype(BF16)
    hi32 = hi.astype(F32)
    lo = (hn - hi32).astype(BF16)
    r = (jnp.dot(hi, wr_ref[...], preferred_element_type=F32)
         + jnp.dot(lo, wr0_ref[...], preferred_element_type=F32))
    lg_ref[...] = r + pltpu.roll(r, LANES - N_EXPERTS, 1)
    h2p_ref[...] = _pack_bf16_pairs(hi32)


def _mix(q, k, v, u, x, bias_tabs, w_pool, pool_scale, w_out, norm_ffn, wr, wr0, *, head_dim):
    b, t, attn = q.shape
    d = x.shape[-1]
    pool = u.shape[-1]
    n_heads = attn // head_dim
    pool_ch = pool // len(POOL_WINDOWS)
    tq = ROWS_PER_STEP * GRID_W
    nb = t // tq
    assert 2 * ROWS_PER_STEP == WIN_R and t % tq == 0 and nb >= 2
    hb = tq // POOL_HALO

    cur = lambda bi, i: (bi, i, 0)
    prev = lambda bi, i: (bi, jnp.maximum(i - 1, 0), 0)
    nxt = lambda bi, i: (bi, jnp.minimum(i + 1, nb - 1), 0)
    c2 = lambda bi, i: (0, 0)
    c3 = lambda bi, i: (0, 0, 0)
    kv = lambda im: pl.BlockSpec((None, tq, attn), im)
    return pl.pallas_call(
        functools.partial(_mix_body, seq_len=t, head_dim=head_dim, n_heads=n_heads, pool_ch=pool_ch),
        grid=(b, nb),
        in_specs=[
            kv(cur), kv(prev), kv(cur), kv(nxt), kv(prev), kv(cur), kv(nxt),
            pl.BlockSpec((None, n_heads, tq, 3 * tq),
                         lambda bi, i: (jnp.where(i == 0, 0, jnp.where(i == nb - 1, 2, 1)), 0, 0, 0)),
            pl.BlockSpec((None, tq, pool), cur),
            pl.BlockSpec((None, POOL_HALO, pool), lambda bi, i: (bi, jnp.maximum(i * hb - 1, 0), 0)),
            pl.BlockSpec((None, POOL_HALO, pool),
                         lambda bi, i: (bi, jnp.minimum((i + 1) * hb, t // POOL_HALO - 1), 0)),
            pl.BlockSpec((None, tq, d), cur),
            pl.BlockSpec(w_pool.shape, c3),
            pl.BlockSpec((1, pool), c2),
            pl.BlockSpec(w_out.shape, c2, pipeline_mode=pl.Buffered(1)),
            pl.BlockSpec((1, d), c2),
            pl.BlockSpec(wr.shape, c2),
            pl.BlockSpec(wr0.shape, c2),
        ],
        out_specs=[
            pl.BlockSpec((None, tq, d), cur),
            pl.BlockSpec((None, tq, d // 2), cur),
            pl.BlockSpec((None, tq, LANES), cur),
        ],
        out_shape=[
            jax.ShapeDtypeStruct((b, t, d), F32),
            jax.ShapeDtypeStruct((b, t, d // 2), U32),
            jax.ShapeDtypeStruct((b, t, LANES), F32),
        ],
        scratch_shapes=[pltpu.VMEM((tq, attn + pool), BF16)],
        compiler_params=pltpu.CompilerParams(
            dimension_semantics=("parallel", "arbitrary"), vmem_limit_bytes=VMEM_LIMIT),
        name="mix",
    )(q, k, k, k, v, v, v, bias_tabs, u, u, u, x, w_pool, pool_scale, w_out, norm_ffn, wr, wr0)


def _route_body(lg_ref, idx_ref, dst_ref, wt_ref, off_ref, aff_ref, sel_ref, rank_ref, thr_ref, *, cap):
    n_exp, tl, nblk = lg_ref.shape

    m = lg_ref[0]
    for e in range(1, n_exp):
        m = jnp.maximum(m, lg_ref[e])
    den = jnp.zeros_like(m)
    for e in range(n_exp):
        ex = jnp.exp(lg_ref[e] - m)
        aff_ref[e] = ex
        den = den + ex
    for e in range(n_exp):
        aff_ref[e] = aff_ref[e] / den

    def total(x):
        return jnp.sum(jnp.sum(x, axis=0, keepdims=True), axis=1, keepdims=True)

    def search(it, thr):
        bit = jnp.left_shift(jnp.int32(1), 30 - it)
        out = []
        for e in range(n_exp):
            bits = lax.bitcast_convert_type(aff_ref[e], I32)
            cand = thr[e] | bit
            c = total(jnp.where(bits >= cand, 1.0, 0.0))
            out.append(jnp.where(c >= cap, cand, thr[e]))
        return tuple(out)

    thr = lax.fori_loop(0, 31, search, tuple(jnp.zeros((1, 1), I32) for _ in range(n_exp)))
    for e in range(n_exp):
        thr_ref[e] = jnp.broadcast_to(thr[e], (SUBLANES, LANES))

    def tri():
        r_i = lax.broadcasted_iota(I32, (tl, tl), 0)
        c_i = lax.broadcasted_iota(I32, (tl, tl), 1)
        return jnp.where(c_i <= r_i, 1.0, 0.0).astype(BF16)

    def before(col_form=False):
        rb = lax.broadcasted_iota(I32, (nblk, nblk), 0)
        cb = lax.broadcasted_iota(I32, (nblk, nblk), 1)
        return jnp.where((cb < rb) if col_form else (rb < cb), 1.0, 0.0).astype(BF16)

    def prefix(mask):
        incl = jnp.dot(tri(), mask.astype(BF16), preferred_element_type=F32)
        tot8 = jnp.broadcast_to(incl[tl - 1:tl, :], (SUBLANES, nblk)).astype(BF16)
        base = jnp.dot(tot8, before(), preferred_element_type=F32)[0:1, :]
        return incl, base

    for r in range(n_exp):
        wt_ref[r] = jnp.zeros((tl, nblk), F32)

    def select(e, rank):
        aff = aff_ref[e]
        bits = lax.bitcast_convert_type(aff, I32)
        thr_e = thr_ref[e][0:1, 0:1]
        gt = bits > thr_e
        eq = bits == thr_e
        need = cap - total(jnp.where(gt, 1.0, 0.0))
        eqf = jnp.where(eq, 1.0, 0.0)
        incl_eq, base_eq = prefix(eqf)
        eq_rank = base_eq + incl_eq - eqf
        sel = gt | (eq & (eq_rank < need))
        self_ = jnp.where(sel, 1.0, 0.0)
        for r in range(n_exp):
            wt_ref[r] = jnp.where(sel & (rank == r), aff, wt_ref[r])
        sel_ref[e] = self_
        rank_ref[e] = rank
        return rank + self_

    rank = lax.fori_loop(0, n_exp, select, jnp.zeros((tl, nblk), F32))

    incl_n = jnp.dot(tri(), rank.astype(BF16), preferred_element_type=F32)
    tot_n = jnp.broadcast_to(incl_n[tl - 1:tl, :], (SUBLANES, nblk))
    tot_hi = jnp.floor(tot_n * (1.0 / DIGIT))
    tot_lo = tot_n - tot_hi * DIGIT
    base_n = (DIGIT * jnp.dot(tot_hi.astype(BF16), before(), preferred_element_type=F32)
              + jnp.dot(tot_lo.astype(BF16), before(), preferred_element_type=F32))[0:1, :]
    off_ref[...] = base_n + incl_n - rank

    def place(e, carry):
        slot_f = lax.broadcasted_iota(I32, (nblk, cap), 1).astype(F32)
        blk_f = lax.broadcasted_iota(I32, (nblk, cap), 0).astype(F32)
        tl_f = lax.broadcasted_iota(I32, (tl, cap), 0).astype(F32)
        self_ = sel_ref[e]
        incl, _ = prefix(self_)
        tot_c = jnp.sum(self_.T, axis=1, keepdims=True)
        base_c = jnp.dot(before(col_form=True), jnp.broadcast_to(tot_c, (nblk, LANES)).astype(BF16),
                         preferred_element_type=F32)[:, 0:1]
        owns = (base_c <= slot_f) & (slot_f < base_c + tot_c)
        local = jnp.sum(jnp.where(owns, slot_f - base_c, 0.0), axis=0, keepdims=True)
        blk = jnp.sum(jnp.where(owns, blk_f, 0.0), axis=0, keepdims=True)
        owns_b = jnp.where(owns, 1.0, 0.0).astype(BF16)
        incl_at = jnp.dot(incl.astype(BF16), owns_b, preferred_element_type=F32)
        pos = jnp.sum(jnp.where(incl_at <= local, 1.0, 0.0), axis=0, keepdims=True)
        idx_ref[e] = (blk * tl + pos).astype(I32)

        row = off_ref[...] + rank_ref[e]
        d2 = jnp.floor(row * (1.0 / (DIGIT * DIGIT)))
        rem = row - d2 * (DIGIT * DIGIT)
        d1 = jnp.floor(rem * (1.0 / DIGIT))
        d0 = rem - d1 * DIGIT
        at = lambda dg: jnp.dot(dg.astype(BF16), owns_b, preferred_element_type=F32)
        row_at = (DIGIT * DIGIT) * at(d2) + DIGIT * at(d1) + at(d0)
        dst_ref[e] = jnp.sum(jnp.where(tl_f == pos, row_at, 0.0), axis=0, keepdims=True).astype(I32)
        return carry

    lax.fori_loop(0, n_exp, place, 0)


def _route(lg, cap):
    n_exp, tl, nblk = lg.shape
    full = lambda: pl.BlockSpec(memory_space=pltpu.VMEM)
    return pl.pallas_call(
        functools.partial(_route_body, cap=cap),
        in_specs=[full()],
        out_specs=[full() for _ in range(4)],
        out_shape=[
            jax.ShapeDtypeStruct((n_exp, 1, cap), I32),
            jax.ShapeDtypeStruct((n_exp, 1, cap), I32),
            jax.ShapeDtypeStruct((n_exp, tl, nblk), F32),
            jax.ShapeDtypeStruct((tl, nblk), F32),
        ],
        scratch_shapes=[pltpu.VMEM((n_exp, tl, nblk), F32) for _ in range(3)]
                       + [pltpu.VMEM((n_exp, SUBLANES, LANES), I32)],
        compiler_params=pltpu.CompilerParams(vmem_limit_bytes=VMEM_LIMIT),
        name="route",
    )(lg)


def _expert_body(idx_ref, nxt_ref, prev_dst_ref, dst_ref, h_ref, wg_ref, wu_ref, wd_ref, z_ref,
                 rows, x_scr, acc_ref, y_scr, sem, out_sem, *, tc):
    e, c, f = pl.program_id(0), pl.program_id(1), pl.program_id(2)
    tile = e * pl.num_programs(1) + c
    n_tiles = pl.num_programs(0) * pl.num_programs(1)
    last_f = pl.num_programs(2) - 1
    padded, half = rows.shape
    per_step = padded // EXPERT_FF_STEPS

    def gather_row(tok_ref, r):
        return pltpu.make_async_copy(h_ref.at[pl.ds(tok_ref[0, r], 1)], rows.at[pl.ds(r, 1)], sem)

    def scatter_row(row_ref, r):
        return pltpu.make_async_copy(y_scr.at[pl.ds(r, 1)], z_ref.at[pl.ds(row_ref[0, r], 1)], out_sem)

    def all_gathered():
        return pltpu.make_async_copy(h_ref.at[pl.ds(0, padded)], rows, sem)

    def all_scattered(n_rows):
        return pltpu.make_async_copy(y_scr.at[pl.ds(0, n_rows)], z_ref.at[pl.ds(0, n_rows)], out_sem)

    def in_groups(n_rows, start_row):
        def group(g, carry):
            first = pl.multiple_of(g * ROW_UNROLL, ROW_UNROLL)
            for k in range(ROW_UNROLL):
                start_row(first + k)
            return carry
        lax.fori_loop(0, n_rows // ROW_UNROLL, group, 0)

    @pl.when((tile == 0) & (f == 0))
    def _():
        y_scr[...] = jnp.zeros_like(y_scr)
        in_groups(padded, lambda r: gather_row(idx_ref, r).start())

    @pl.when(f == 0)
    def _():
        all_gathered().wait()
        hi, lo = _unpack_bf16_pairs(rows[0:tc, :])
        x_scr[:, :half] = hi.astype(BF16)
        x_scr[:, half:] = lo.astype(BF16)
        acc_ref[...] = jnp.zeros_like(acc_ref)

    for g in range(per_step // ROW_UNROLL):
        first = pl.multiple_of(f * per_step + g * ROW_UNROLL, ROW_UNROLL)
        for k in range(ROW_UNROLL):
            gather_row(nxt_ref, first + k).start()
            scatter_row(prev_dst_ref, first + k).start()

    x = x_scr[...]
    a = jnp.dot(x, wg_ref[...].astype(BF16), preferred_element_type=F32)
    b = jnp.dot(x, wu_ref[...].astype(BF16), preferred_element_type=F32)
    hmid = (a * jax.nn.sigmoid(a) * b).astype(BF16)
    acc_ref[...] += jnp.dot(hmid, wd_ref[...].astype(BF16), preferred_element_type=F32)

    @pl.when(f == last_f)
    def _():
        all_scattered(padded).wait()
        y_scr[0:tc, :] = _pack_bf16_pairs(acc_ref[...].astype(BF16).astype(F32))

        @pl.when(tile == n_tiles - 1)
        def _():
            in_groups(tc, lambda r: scatter_row(dst_ref, r).start())
            all_scattered(tc).wait()
            all_gathered().wait()


def _experts(idx, dst, h2p, w_gate, w_up, w_down, cap):
    n_exp, d, ff = w_gate.shape
    half = h2p.shape[1]
    tc = idx.shape[-1]
    tf = ff // EXPERT_FF_STEPS
    ct = cap // tc
    n_tiles = n_exp * ct
    assert cap % tc == 0 and ff % EXPERT_FF_STEPS == 0 and tf % LANES == 0 and tc % ROW_UNROLL == 0
    per_step = -(-tc // (EXPERT_FF_STEPS * ROW_UNROLL)) * ROW_UNROLL
    padded = per_step * EXPERT_FF_STEPS
    spare = padded - tc
    idx = jnp.pad(idx, ((0, 0), (0, 0), (0, spare)))
    dst = jnp.concatenate(
        [dst, jnp.broadcast_to(n_exp * cap + jnp.arange(spare, dtype=I32), (n_tiles, 1, spare))], axis=2)
    smem = lambda im: pl.BlockSpec((None, 1, padded), im, memory_space=pltpu.SMEM)
    return pl.pallas_call(
        functools.partial(_expert_body, tc=tc),
        grid=(n_exp, ct, EXPERT_FF_STEPS),
        in_specs=[
            smem(lambda e, c, f: (e * ct + c, 0, 0)),
            smem(lambda e, c, f: (jnp.minimum(e * ct + c + 1, n_tiles - 1), 0, 0)),
            smem(lambda e, c, f: (jnp.maximum(e * ct + c - 1, 0), 0, 0)),
            smem(lambda e, c, f: (e * ct + c, 0, 0)),
            pl.BlockSpec(memory_space=pl.ANY),
            pl.BlockSpec((None, d, tf), lambda e, c, f: (e, 0, f)),
            pl.BlockSpec((None, d, tf), lambda e, c, f: (e, 0, f)),
            pl.BlockSpec((None, tf, d), lambda e, c, f: (e, f, 0)),
        ],
        out_specs=pl.BlockSpec(memory_space=pl.ANY),
        out_shape=jax.ShapeDtypeStruct((n_exp * cap + max(spare, SUBLANES), half), U32),
        scratch_shapes=[pltpu.VMEM((padded, half), U32), pltpu.VMEM((tc, d), BF16), pltpu.VMEM((tc, d), F32),
                        pltpu.VMEM((padded, half), U32), pltpu.SemaphoreType.DMA(()),
                        pltpu.SemaphoreType.DMA(())],
        compiler_params=pltpu.CompilerParams(
            dimension_semantics=("arbitrary", "arbitrary", "arbitrary"), vmem_limit_bytes=VMEM_LIMIT,
            has_side_effects=True),
        name="experts",
    )(idx, idx, dst, dst, h2p, w_gate, w_up, w_down)


def _combine_body(start_ref, x2_ref, off_ref, wt_ref, z_ref, o_ref, buf, sem):
    i = pl.program_id(0)
    nsteps = pl.num_programs(0)
    _, _, rc, half = buf.shape
    n_rows = z_ref.shape[0]

    def first_row(b):
        return (start_ref[b] // SUBLANES) * SUBLANES

    def n_chunks(b):
        return (start_ref[b + 1] - first_row(b) + rc - 1) // rc

    def chunk_copy(b, k, slot):
        row = pl.multiple_of(jnp.minimum(first_row(b) + k * rc, n_rows - rc), SUBLANES)
        return pltpu.make_async_copy(z_ref.at[pl.ds(row, rc)], buf.at[slot, k], sem.at[slot])

    def fetch(b, slot):
        def one(k, carry):
            chunk_copy(b, k, slot).start()
            return carry
        lax.fori_loop(0, n_chunks(b), one, 0)

    @pl.when(i == 0)
    def _():
        fetch(0, 0)

    @pl.when(i + 1 < nsteps)
    def _():
        fetch(i + 1, (i + 1) % 2)

    slot = i % 2
    o_ref[...] = x2_ref[...]
    tb = o_ref.shape[0]
    off = off_ref[...]
    sub = lax.broadcasted_iota(I32, (rc, 1), 0)

    def landed(k, carry):
        chunk_copy(i, k, slot).wait()
        return carry

    lax.fori_loop(0, n_chunks(i), landed, 0)

    def chunk(k, carry):
        want = first_row(i) + k * rc
        row = jnp.minimum(want, n_rows - rc) + sub
        rel = jnp.where(row >= want, row.astype(F32) - off, -1.0)
        a = jnp.zeros(rel.shape, F32)
        for r in range(wt_ref.shape[0]):
            a = jnp.where(rel == r, wt_ref[r:r + 1, :], a)
        a_hi = a.astype(BF16)
        a_lo = (a - a_hi.astype(F32)).astype(BF16)
        a2 = jnp.concatenate([a_hi, a_lo], axis=1)
        z_hi, z_lo = _unpack_bf16_pairs(buf[slot, k])

        def seg(zz):
            s = lax.dot_general(a2, zz.astype(BF16), (((0,), (0,)), ((), ())), preferred_element_type=F32)
            return s[:tb] + s[tb:]
        o_ref[:, :half] += seg(z_hi)
        o_ref[:, half:] += seg(z_lo)
        return carry

    lax.fori_loop(0, n_chunks(i), chunk, 0)


def _combine(starts, x2, off, wt, z):
    n, d = x2.shape
    tb = COMBINE_TOKENS
    rc = COMBINE_ROWS
    max_chunks = (N_EXPERTS * tb + SUBLANES - 1 + rc - 1) // rc + 1
    assert n % tb == 0 and z.shape[0] % SUBLANES == 0 and z.shape[0] >= rc
    return pl.pallas_call(
        _combine_body,
        grid_spec=pltpu.PrefetchScalarGridSpec(
            num_scalar_prefetch=1,
            grid=(n // tb,),
            in_specs=[
                pl.BlockSpec((tb, d), lambda i, s: (i, 0)),
                pl.BlockSpec((1, tb), lambda i, s: (0, i)),
                pl.BlockSpec((N_EXPERTS, tb), lambda i, s: (0, i)),
                pl.BlockSpec(memory_space=pl.ANY),
            ],
            out_specs=pl.BlockSpec((tb, d), lambda i, s: (i, 0)),
            scratch_shapes=[pltpu.VMEM((2, max_chunks, rc, d // 2), U32), pltpu.SemaphoreType.DMA((2,))],
        ),
        out_shape=jax.ShapeDtypeStruct((n, d), F32),
        compiler_params=pltpu.CompilerParams(
            dimension_semantics=("arbitrary",), vmem_limit_bytes=VMEM_LIMIT),
        name="combine",
    )(starts, x2, off, wt, z)


def _layer(x, p):
    b, t, d = x.shape
    n = b * t
    head_dim = p["q_norm"].shape[-1]
    attn = N_HEADS * head_dim
    q, k, v, u = _in_proj(x.reshape(n, d), p["norm_mix"], p["w_in"], p["q_norm"], p["k_norm"],
                          attn=attn, head_dim=head_dim)
    shp = lambda a: a.reshape(b, t, a.shape[-1])
    x2, h2p, lg = _mix(shp(q), shp(k), shp(v), shp(u), x, p["bias_tabs"], p["w_pool"], p["pool_scale"],
                       p["w_out"], p["norm_ffn"], p["wr"], p["wr0"], head_dim=head_dim)

    nblk = n // LANES
    cap = max(1, (CAPACITY_FACTOR * n) // N_EXPERTS)
    lg = lg.reshape(n, LANES)[:, :N_EXPERTS]
    lg = lg.T.reshape(N_EXPERTS, nblk, LANES).transpose(0, 2, 1)
    idx, dst, wt, off = _route(lg, cap)

    tc = min(EXPERT_TOKENS, cap)
    tiles = lambda a: a.reshape(N_EXPERTS * (cap // tc), 1, tc)
    z = _experts(tiles(idx), tiles(dst), h2p.reshape(n, d // 2), p["w_gate"], p["w_up"], p["w_down"], cap)

    assert COMBINE_TOKENS == LANES
    starts = jnp.concatenate([off[0, :].astype(I32), jnp.full((1,), N_EXPERTS * cap, I32)])
    wt = wt.transpose(0, 2, 1).reshape(N_EXPERTS, n)
    out = _combine(starts, x2.reshape(n, d), off.T.reshape(1, n), wt, z)
    return out.reshape(b, t, d)


def _prepare(l, norm_mix, w_in, q_norm, k_norm, rel_pos_bias, w_pool, pool_scale, w_out, norm_ffn,
             w_router, w_gate, w_up, w_down):
    d = w_in.shape[1]
    whi = w_router[l].astype(BF16)
    wlo = (w_router[l] - whi.astype(F32)).astype(BF16)
    pad = lambda *cols: jnp.concatenate(
        list(cols) + [jnp.zeros((d, LANES - sum(c.shape[1] for c in cols)), BF16)], axis=1)
    return dict(
        norm_mix=norm_mix[l][None, :], w_in=w_in[l].astype(BF16),
        q_norm=q_norm[l][None, :], k_norm=k_norm[l][None, :],
        bias_tabs=_bias_tables(rel_pos_bias[l]), w_pool=w_pool[l].astype(BF16),
        pool_scale=pool_scale[l][None, :], w_out=w_out[l].astype(BF16), norm_ffn=norm_ffn[l][None, :],
        wr=pad(whi, wlo), wr0=pad(whi),
        w_gate=w_gate[l], w_up=w_up[l], w_down=w_down[l],
    )


def kernel(x_prompt, x_sample, norm_mix, w_in, q_norm, k_norm, rel_pos_bias, w_pool, pool_scale, w_out,
           norm_ffn, w_router, w_gate, w_up, w_down):
    weights = (norm_mix, w_in, q_norm, k_norm, rel_pos_bias, w_pool, pool_scale, w_out, norm_ffn,
               w_router, w_gate, w_up, w_down)
    layers = [_prepare(l, *weights) for l in range(w_in.shape[0])]

    def trunk(x):
        for p in layers:
            x = _layer(x, p)
        return x

    return trunk(x_prompt), trunk(x_sample)
```

```python
import functools

import numpy as np
import jax
import jax.numpy as jnp
from jax import lax
from jax.experimental import pallas as pl
from jax.experimental.pallas import tpu as pltpu

F32 = jnp.float32
BF16 = jnp.bfloat16
I32 = jnp.int32
U32 = jnp.uint32

GRID_W = 64
WIN_R = 8
WIN_C = 16
N_HEADS = 8
POOL_WINDOWS = (2, 4, 8, 16)
N_EXPERTS = 16
CAPACITY_FACTOR = 2
EPS = 1e-6

LANES = 128
SUBLANES = 8
MASKED = -1e30
VMEM_LIMIT = 56 * 1024 * 1024
HI16 = 0xFFFF0000
DIGIT = 64.0

ROWS_PER_STEP = 4
POOL_HALO = 8
IN_PROJ_TOKENS = 512
EXPERT_TOKENS = 1024
EXPERT_FF_STEPS = 11
ROW_UNROLL = 8
COMBINE_TOKENS = 128
COMBINE_ROWS = 128


def _pack_bf16_pairs(x):
    b = lax.bitcast_convert_type(x, U32)
    h = x.shape[1] // 2
    return b[:, :h] | (b[:, h:] >> 16)


def _unpack_bf16_pairs(w):
    hi = lax.bitcast_convert_type(w & jnp.uint32(HI16), F32)
    lo = lax.bitcast_convert_type(w << 16, F32)
    return hi, lo


def _in_proj_body(x_ref, g_ref, w_ref, qn_ref, kn_ref, q_ref, k_ref, v_ref, u_ref, *, attn, head_dim):
    x = x_ref[...]
    h = x * lax.rsqrt(jnp.mean(x * x, axis=-1, keepdims=True) + EPS) * g_ref[...]
    h = h.astype(BF16)

    def head_norm(z, gain, scale, out_ref):
        for hd in range(attn // head_dim):
            sl = slice(hd * head_dim, (hd + 1) * head_dim)
            zh = z[:, sl]
            r = lax.rsqrt(jnp.mean(zh * zh, axis=-1, keepdims=True) + EPS)
            out_ref[:, sl] = (zh * r * gain * scale).astype(BF16)

    zq = jnp.dot(h, w_ref[:, 0:attn], preferred_element_type=F32)
    head_norm(zq, qn_ref[...], head_dim ** -0.5, q_ref)
    zk = jnp.dot(h, w_ref[:, attn:2 * attn], preferred_element_type=F32)
    head_norm(zk, kn_ref[...], 1.0, k_ref)
    v_ref[...] = jnp.dot(h, w_ref[:, 2 * attn:3 * attn], preferred_element_type=F32).astype(BF16)
    u_ref[...] = jnp.dot(h, w_ref[:, 3 * attn:], preferred_element_type=F32)


def _in_proj(x, gain, w_in, q_norm, k_norm, *, attn, head_dim):
    n, d = x.shape
    width = w_in.shape[1]
    pool = width - 3 * attn
    tm = min(IN_PROJ_TOKENS, n)
    const = lambda i: (0, 0)
    tok = lambda i: (i, 0)
    return pl.pallas_call(
        functools.partial(_in_proj_body, attn=attn, head_dim=head_dim),
        grid=(n // tm,),
        in_specs=[
            pl.BlockSpec((tm, d), tok),
            pl.BlockSpec((1, d), const),
            pl.BlockSpec((d, width), const, pipeline_mode=pl.Buffered(1)),
            pl.BlockSpec((1, head_dim), const),
            pl.BlockSpec((1, head_dim), const),
        ],
        out_specs=[
            pl.BlockSpec((tm, attn), tok),
            pl.BlockSpec((tm, attn), tok),
            pl.BlockSpec((tm, attn), tok),
            pl.BlockSpec((tm, pool), tok),
        ],
        out_shape=[
            jax.ShapeDtypeStruct((n, attn), BF16),
            jax.ShapeDtypeStruct((n, attn), BF16),
            jax.ShapeDtypeStruct((n, attn), BF16),
            jax.ShapeDtypeStruct((n, pool), F32),
        ],
        compiler_params=pltpu.CompilerParams(
            dimension_semantics=("parallel",), vmem_limit_bytes=VMEM_LIMIT),
        name="in_proj",
    )(x, gain, w_in, q_norm, k_norm)


def _bias_tables(rpb):
    rq = ROWS_PER_STEP
    nk = 3 * rq
    qi = np.arange(rq)[:, None]
    kj = np.arange(nk)[None, :]
    d = kj - rq - qi
    dr = np.clip(d + WIN_R - 1, 0, 2 * WIN_R - 2)
    ones = np.ones((rq, nk), bool)
    valid_first = ones & (kj >= rq) & (kj - rq < WIN_R)
    valid_mid = (d >= -(WIN_R // 2)) & (d < WIN_R // 2)
    valid_last = ones & (kj < 2 * rq)
    c = np.arange(GRID_W)[:, None]
    kc = np.arange(GRID_W)[None, :]
    cs = np.clip(c - WIN_C // 2, 0, GRID_W - WIN_C)
    valid_c = (kc >= cs) & (kc < cs + WIN_C)
    padded = jnp.pad(rpb, ((0, 0), (0, 0), (GRID_W, GRID_W)))
    first = GRID_W + WIN_C - 1
    by_col = jnp.stack([padded[:, :, first - ci:first - ci + GRID_W] for ci in range(GRID_W)],
                       axis=2)
    vals = jnp.stack([jnp.stack([by_col[:, dr[a, b]] for b in range(nk)], axis=2) for a in range(rq)],
                     axis=1)
    tabs = []
    for vr in (valid_first, valid_mid, valid_last):
        ok = vr[:, None, :, None] & valid_c[None, :, None, :]
        t = jnp.where(ok[None], vals, MASKED)
        tabs.append(t.reshape(rpb.shape[0], rq * GRID_W, nk * GRID_W))
    return jnp.stack(tabs).astype(F32)


def _mix_body(q_ref, kp_ref, kc_ref, kn_ref, vp_ref, vc_ref, vn_ref, bt_ref, uc_ref, up_ref, un_ref, x_ref,
              wpool_ref, pscale_ref, wout_ref, nffn_ref, wr_ref, wr0_ref,
              x2_ref, h2p_ref, lg_ref, mix_ref, *, seq_len, head_dim, n_heads, pool_ch):
    i = pl.program_id(1)
    nb = pl.num_programs(1)
    tq = q_ref.shape[0]
    attn = n_heads * head_dim

    kcat = jnp.concatenate([kp_ref[...], kc_ref[...], kn_ref[...]], axis=0)
    vcat = jnp.concatenate([vp_ref[...], vc_ref[...], vn_ref[...]], axis=0)
    for h in range(n_heads):
        sl = slice(h * head_dim, (h + 1) * head_dim)
        s = lax.dot_general(q_ref[:, sl], kcat[:, sl], (((1,), (1,)), ((), ())),
                            preferred_element_type=F32)
        s = s + bt_ref[h]
        m = jnp.max(s, axis=-1, keepdims=True)
        p = jnp.exp(s - m)
        l = jnp.sum(p, axis=-1, keepdims=True)
        o = jnp.dot(p.astype(BF16), vcat[:, sl], preferred_element_type=F32)
        mix_ref[:, sl] = (o / l).astype(BF16)

    ucur = uc_ref[...]
    up = jnp.where(i == 0, 0.0, up_ref[...])
    un = jnp.where(i == nb - 1, 0.0, un_ref[...])
    uext = jnp.concatenate([up, ucur, un], axis=0)
    t = i * tq + lax.broadcasted_iota(I32, (tq, 1), 0)
    for g, w in enumerate(POOL_WINDOWS):
        gs = slice(g * pool_ch, (g + 1) * pool_ch)
        ug = uext[:, gs]
        arr = ug[:-1] + ug[1:]
        sh = 2
        while sh < w:
            arr = arr[:-sh] + arr[sh:]
            sh *= 2
        start = POOL_HALO - w // 2
        wsum = arr[start:start + tq]
        cnt = (jnp.minimum(t + w // 2, seq_len) - jnp.maximum(t - w // 2, 0)).astype(F32)
        pg = (wsum / cnt - ucur[:, gs]).astype(BF16)
        yg = jnp.dot(pg, wpool_ref[g], preferred_element_type=F32) * pscale_ref[:, gs]
        mix_ref[:, attn + g * pool_ch:attn + (g + 1) * pool_ch] = yg.astype(BF16)

    x2 = x_ref[...] + jnp.dot(mix_ref[...], wout_ref[...], preferred_element_type=F32)
    x2_ref[...] = x2
    hn = x2 * lax.rsqrt(jnp.mean(x2 * x2, axis=-1, keepdims=True) + EPS) * nffn_ref[...]
    hi = hn.astype(BF16)
    hi32 = hi.astype(F32)
    lo = (hn - hi32).astype(BF16)
    r = (jnp.dot(hi, wr_ref[...], preferred_element_type=F32)
         + jnp.dot(lo, wr0_ref[...], preferred_element_type=F32))
    lg_ref[...] = r + pltpu.roll(r, LANES - N_EXPERTS, 1)
    h2p_ref[...] = _pack_bf16_pairs(hi32)


def _mix(q, k, v, u, x, bias_tabs, w_pool, pool_scale, w_out, norm_ffn, wr, wr0, *, head_dim):
    b, t, attn = q.shape
    d = x.shape[-1]
    pool = u.shape[-1]
    n_heads = attn // head_dim
    pool_ch = pool // len(POOL_WINDOWS)
    tq = ROWS_PER_STEP * GRID_W
    nb = t // tq
    assert 2 * ROWS_PER_STEP == WIN_R and t % tq == 0 and nb >= 2
    hb = tq // POOL_HALO

    cur = lambda bi, i: (bi, i, 0)
    prev = lambda bi, i: (bi, jnp.maximum(i - 1, 0), 0)
    nxt = lambda bi, i: (bi, jnp.minimum(i + 1, nb - 1), 0)
    c2 = lambda bi, i: (0, 0)
    c3 = lambda bi, i: (0, 0, 0)
    kv = lambda im: pl.BlockSpec((None, tq, attn), im)
    return pl.pallas_call(
        functools.partial(_mix_body, seq_len=t, head_dim=head_dim, n_heads=n_heads, pool_ch=pool_ch),
        grid=(b, nb),
        in_specs=[
            kv(cur), kv(prev), kv(cur), kv(nxt), kv(prev), kv(cur), kv(nxt),
            pl.BlockSpec((None, n_heads, tq, 3 * tq),
                         lambda bi, i: (jnp.where(i == 0, 0, jnp.where(i == nb - 1, 2, 1)), 0, 0, 0)),
            pl.BlockSpec((None, tq, pool), cur),
            pl.BlockSpec((None, POOL_HALO, pool), lambda bi, i: (bi, jnp.maximum(i * hb - 1, 0), 0)),
            pl.BlockSpec((None, POOL_HALO, pool),
                         lambda bi, i: (bi, jnp.minimum((i + 1) * hb, t // POOL_HALO - 1), 0)),
            pl.BlockSpec((None, tq, d), cur),
            pl.BlockSpec(w_pool.shape, c3),
            pl.BlockSpec((1, pool), c2),
            pl.BlockSpec(w_out.shape, c2, pipeline_mode=pl.Buffered(1)),
            pl.BlockSpec((1, d), c2),
            pl.BlockSpec(wr.shape, c2),
            pl.BlockSpec(wr0.shape, c2),
        ],
        out_specs=[
            pl.BlockSpec((None, tq, d), cur),
            pl.BlockSpec((None, tq, d // 2), cur),
            pl.BlockSpec((None, tq, LANES), cur),
        ],
        out_shape=[
            jax.ShapeDtypeStruct((b, t, d), F32),
            jax.ShapeDtypeStruct((b, t, d // 2), U32),
            jax.ShapeDtypeStruct((b, t, LANES), F32),
        ],
        scratch_shapes=[pltpu.VMEM((tq, attn + pool), BF16)],
        compiler_params=pltpu.CompilerParams(
            dimension_semantics=("parallel", "arbitrary"), vmem_limit_bytes=VMEM_LIMIT),
        name="mix",
    )(q, k, k, k, v, v, v, bias_tabs, u, u, u, x, w_pool, pool_scale, w_out, norm_ffn, wr, wr0)


def _route_body(lg_ref, idx_ref, dst_ref, wt_ref, off_ref, aff_ref, sel_ref, rank_ref, thr_ref, *, cap):
    n_exp, tl, nblk = lg_ref.shape

    m = lg_ref[0]
    for e in range(1, n_exp):
        m = jnp.maximum(m, lg_ref[e])
    den = jnp.zeros_like(m)
    for e in range(n_exp):
        ex = jnp.exp(lg_ref[e] - m)
        aff_ref[e] = ex
        den = den + ex
    for e in range(n_exp):
        aff_ref[e] = aff_ref[e] / den

    def total(x):
        return jnp.sum(jnp.sum(x, axis=0, keepdims=True), axis=1, keepdims=True)

    def search(it, thr):
        bit = jnp.left_shift(jnp.int32(1), 30 - it)
        out = []
        for e in range(n_exp):
            bits = lax.bitcast_convert_type(aff_ref[e], I32)
            cand = thr[e] | bit
            c = total(jnp.where(bits >= cand, 1.0, 0.0))
            out.append(jnp.where(c >= cap, cand, thr[e]))
        return tuple(out)

    thr = lax.fori_loop(0, 31, search, tuple(jnp.zeros((1, 1), I32) for _ in range(n_exp)))
    for e in range(n_exp):
        thr_ref[e] = jnp.broadcast_to(thr[e], (SUBLANES, LANES))

    def tri():
        r_i = lax.broadcasted_iota(I32, (tl, tl), 0)
        c_i = lax.broadcasted_iota(I32, (tl, tl), 1)
        return jnp.where(c_i <= r_i, 1.0, 0.0).astype(BF16)

    def before(col_form=False):
        rb = lax.broadcasted_iota(I32, (nblk, nblk), 0)
        cb = lax.broadcasted_iota(I32, (nblk, nblk), 1)
        return jnp.where((cb < rb) if col_form else (rb < cb), 1.0, 0.0).astype(BF16)

    def prefix(mask):
        incl = jnp.dot(tri(), mask.astype(BF16), preferred_element_type=F32)
        tot8 = jnp.broadcast_to(incl[tl - 1:tl, :], (SUBLANES, nblk)).astype(BF16)
        base = jnp.dot(tot8, before(), preferred_element_type=F32)[0:1, :]
        return incl, base

    for r in range(n_exp):
        wt_ref[r] = jnp.zeros((tl, nblk), F32)

    def select(e, rank):
        aff = aff_ref[e]
        bits = lax.bitcast_convert_type(aff, I32)
        thr_e = thr_ref[e][0:1, 0:1]
        gt = bits > thr_e
        eq = bits == thr_e
        need = cap - total(jnp.where(gt, 1.0, 0.0))
        eqf = jnp.where(eq, 1.0, 0.0)
        incl_eq, base_eq = prefix(eqf)
        eq_rank = base_eq + incl_eq - eqf
        sel = gt | (eq & (eq_rank < need))
        self_ = jnp.where(sel, 1.0, 0.0)
        for r in range(n_exp):
            wt_ref[r] = jnp.where(sel & (rank == r), aff, wt_ref[r])
        sel_ref[e] = self_
        rank_ref[e] = rank
        return rank + self_

    rank = lax.fori_loop(0, n_exp, select, jnp.zeros((tl, nblk), F32))

    incl_n = jnp.dot(tri(), rank.astype(BF16), preferred_element_type=F32)
    tot_n = jnp.broadcast_to(incl_n[tl - 1:tl, :], (SUBLANES, nblk))
    tot_hi = jnp.floor(tot_n * (1.0 / DIGIT))
    tot_lo = tot_n - tot_hi * DIGIT
    base_n = (DIGIT * jnp.dot(tot_hi.astype(BF16), before(), preferred_element_type=F32)
              + jnp.dot(tot_lo.astype(BF16), before(), preferred_element_type=F32))[0:1, :]
    off_ref[...] = base_n + incl_n - rank

    def place(e, carry):
        slot_f = lax.broadcasted_iota(I32, (nblk, cap), 1).astype(F32)
        blk_f = lax.broadcasted_iota(I32, (nblk, cap), 0).astype(F32)
        tl_f = lax.broadcasted_iota(I32, (tl, cap), 0).astype(F32)
        self_ = sel_ref[e]
        incl, _ = prefix(self_)
        tot_c = jnp.sum(self_.T, axis=1, keepdims=True)
        base_c = jnp.dot(before(col_form=True), jnp.broadcast_to(tot_c, (nblk, LANES)).astype(BF16),
                         preferred_element_type=F32)[:, 0:1]
        owns = (base_c <= slot_f) & (slot_f < base_c + tot_c)
        local = jnp.sum(jnp.where(owns, slot_f - base_c, 0.0), axis=0, keepdims=True)
        blk = jnp.sum(jnp.where(owns, blk_f, 0.0), axis=0, keepdims=True)
        owns_b = jnp.where(owns, 1.0, 0.0).astype(BF16)
        incl_at = jnp.dot(incl.astype(BF16), owns_b, preferred_element_type=F32)
        pos = jnp.sum(jnp.where(incl_at <= local, 1.0, 0.0), axis=0, keepdims=True)
        idx_ref[e] = (blk * tl + pos).astype(I32)

        row = off_ref[...] + rank_ref[e]
        d2 = jnp.floor(row * (1.0 / (DIGIT * DIGIT)))
        rem = row - d2 * (DIGIT * DIGIT)
        d1 = jnp.floor(rem * (1.0 / DIGIT))
        d0 = rem - d1 * DIGIT
        at = lambda dg: jnp.dot(dg.astype(BF16), owns_b, preferred_element_type=F32)
        row_at = (DIGIT * DIGIT) * at(d2) + DIGIT * at(d1) + at(d0)
        dst_ref[e] = jnp.sum(jnp.where(tl_f == pos, row_at, 0.0), axis=0, keepdims=True).astype(I32)
        return carry

    lax.fori_loop(0, n_exp, place, 0)


def _route(lg, cap):
    n_exp, tl, nblk = lg.shape
    full = lambda: pl.BlockSpec(memory_space=pltpu.VMEM)
    return pl.pallas_call(
        functools.partial(_route_body, cap=cap),
        in_specs=[full()],
        out_specs=[full() for _ in range(4)],
        out_shape=[
            jax.ShapeDtypeStruct((n_exp, 1, cap), I32),
            jax.ShapeDtypeStruct((n_exp, 1, cap), I32),
            jax.ShapeDtypeStruct((n_exp, tl, nblk), F32),
            jax.ShapeDtypeStruct((tl, nblk), F32),
        ],
        scratch_shapes=[pltpu.VMEM((n_exp, tl, nblk), F32) for _ in range(3)]
                       + [pltpu.VMEM((n_exp, SUBLANES, LANES), I32)],
        compiler_params=pltpu.CompilerParams(vmem_limit_bytes=VMEM_LIMIT),
        name="route",
    )(lg)


def _expert_body(idx_ref, nxt_ref, prev_dst_ref, dst_ref, h_ref, wg_ref, wu_ref, wd_ref, z_ref,
                 rows, x_scr, acc_ref, y_scr, sem, out_sem, *, tc):
    e, c, f = pl.program_id(0), pl.program_id(1), pl.program_id(2)
    tile = e * pl.num_programs(1) + c
    n_tiles = pl.num_programs(0) * pl.num_programs(1)
    last_f = pl.num_programs(2) - 1
    padded, half = rows.shape
    per_step = padded // EXPERT_FF_STEPS

    def gather_row(tok_ref, r):
        return pltpu.make_async_copy(h_ref.at[pl.ds(tok_ref[0, r], 1)], rows.at[pl.ds(r, 1)], sem)

    def scatter_row(row_ref, r):
        return pltpu.make_async_copy(y_scr.at[pl.ds(r, 1)], z_ref.at[pl.ds(row_ref[0, r], 1)], out_sem)

    def all_gathered():
        return pltpu.make_async_copy(h_ref.at[pl.ds(0, padded)], rows, sem)

    def all_scattered(n_rows):
        return pltpu.make_async_copy(y_scr.at[pl.ds(0, n_rows)], z_ref.at[pl.ds(0, n_rows)], out_sem)

    def in_groups(n_rows, start_row):
        def group(g, carry):
            first = pl.multiple_of(g * ROW_UNROLL, ROW_UNROLL)
            for k in range(ROW_UNROLL):
                start_row(first + k)
            return carry
        lax.fori_loop(0, n_rows // ROW_UNROLL, group, 0)

    @pl.when((tile == 0) & (f == 0))
    def _():
        y_scr[...] = jnp.zeros_like(y_scr)
        in_groups(padded, lambda r: gather_row(idx_ref, r).start())

    @pl.when(f == 0)
    def _():
        all_gathered().wait()
        hi, lo = _unpack_bf16_pairs(rows[0:tc, :])
        x_scr[:, :half] = hi.astype(BF16)
        x_scr[:, half:] = lo.astype(BF16)
        acc_ref[...] = jnp.zeros_like(acc_ref)

    for g in range(per_step // ROW_UNROLL):
        first = pl.multiple_of(f * per_step + g * ROW_UNROLL, ROW_UNROLL)
        for k in range(ROW_UNROLL):
            gather_row(nxt_ref, first + k).start()
            scatter_row(prev_dst_ref, first + k).start()

    x = x_scr[...]
    a = jnp.dot(x, wg_ref[...].astype(BF16), preferred_element_type=F32)
    b = jnp.dot(x, wu_ref[...].astype(BF16), preferred_element_type=F32)
    hmid = (a * jax.nn.sigmoid(a) * b).astype(BF16)
    acc_ref[...] += jnp.dot(hmid, wd_ref[...].astype(BF16), preferred_element_type=F32)

    @pl.when(f == last_f)
    def _():
        all_scattered(padded).wait()
        y_scr[0:tc, :] = _pack_bf16_pairs(acc_ref[...].astype(BF16).astype(F32))

        @pl.when(tile == n_tiles - 1)
        def _():
            in_groups(tc, lambda r: scatter_row(dst_ref, r).start())
            all_scattered(tc).wait()
            all_gathered().wait()


def _experts(idx, dst, h2p, w_gate, w_up, w_down, cap):
    n_exp, d, ff = w_gate.shape
    half = h2p.shape[1]
    tc = idx.shape[-1]
    tf = ff // EXPERT_FF_STEPS
    ct = cap // tc
    n_tiles = n_exp * ct
    assert cap % tc == 0 and ff % EXPERT_FF_STEPS == 0 and tf % LANES == 0 and tc % ROW_UNROLL == 0
    per_step = -(-tc // (EXPERT_FF_STEPS * ROW_UNROLL)) * ROW_UNROLL
    padded = per_step * EXPERT_FF_STEPS
    spare = padded - tc
    idx = jnp.pad(idx, ((0, 0), (0, 0), (0, spare)))
    dst = jnp.concatenate(
        [dst, jnp.broadcast_to(n_exp * cap + jnp.arange(spare, dtype=I32), (n_tiles, 1, spare))], axis=2)
    smem = lambda im: pl.BlockSpec((None, 1, padded), im, memory_space=pltpu.SMEM)
    return pl.pallas_call(
        functools.partial(_expert_body, tc=tc),
        grid=(n_exp, ct, EXPERT_FF_STEPS),
        in_specs=[
            smem(lambda e, c, f: (e * ct + c, 0, 0)),
            smem(lambda e, c, f: (jnp.minimum(e * ct + c + 1, n_tiles - 1), 0, 0)),
            smem(lambda e, c, f: (jnp.maximum(e * ct + c - 1, 0), 0, 0)),
            smem(lambda e, c, f: (e * ct + c, 0, 0)),
            pl.BlockSpec(memory_space=pl.ANY),
            pl.BlockSpec((None, d, tf), lambda e, c, f: (e, 0, f)),
            pl.BlockSpec((None, d, tf), lambda e, c, f: (e, 0, f)),
            pl.BlockSpec((None, tf, d), lambda e, c, f: (e, f, 0)),
        ],
        out_specs=pl.BlockSpec(memory_space=pl.ANY),
        out_shape=jax.ShapeDtypeStruct((n_exp * cap + max(spare, SUBLANES), half), U32),
        scratch_shapes=[pltpu.VMEM((padded, half), U32), pltpu.VMEM((tc, d), BF16), pltpu.VMEM((tc, d), F32),
                        pltpu.VMEM((padded, half), U32), pltpu.SemaphoreType.DMA(()),
                        pltpu.SemaphoreType.DMA(())],
        compiler_params=pltpu.CompilerParams(
            dimension_semantics=("arbitrary", "arbitrary", "arbitrary"), vmem_limit_bytes=VMEM_LIMIT,
            has_side_effects=True),
        name="experts",
    )(idx, idx, dst, dst, h2p, w_gate, w_up, w_down)


def _combine_body(start_ref, x2_ref, off_ref, wt_ref, z_ref, o_ref, buf, sem):
    i = pl.program_id(0)
    nsteps = pl.num_programs(0)
    _, _, rc, half = buf.shape
    n_rows = z_ref.shape[0]

    def first_row(b):
        return (start_ref[b] // SUBLANES) * SUBLANES

    def n_chunks(b):
        return (start_ref[b + 1] - first_row(b) + rc - 1) // rc

    def chunk_copy(b, k, slot):
        row = pl.multiple_of(jnp.minimum(first_row(b) + k * rc, n_rows - rc), SUBLANES)
        return pltpu.make_async_copy(z_ref.at[pl.ds(row, rc)], buf.at[slot, k], sem.at[slot])

    def fetch(b, slot):
        def one(k, carry):
            chunk_copy(b, k, slot).start()
            return carry
        lax.fori_loop(0, n_chunks(b), one, 0)

    @pl.when(i == 0)
    def _():
        fetch(0, 0)

    @pl.when(i + 1 < nsteps)
    def _():
        fetch(i + 1, (i + 1) % 2)

    slot = i % 2
    o_ref[...] = x2_ref[...]
    off = off_ref[...]
    sub = lax.broadcasted_iota(I32, (rc, 1), 0)

    def landed(k, carry):
        chunk_copy(i, k, slot).wait()
        return carry

    lax.fori_loop(0, n_chunks(i), landed, 0)

    def chunk(k, carry):
        want = first_row(i) + k * rc
        row = jnp.minimum(want, n_rows - rc) + sub
        rel = jnp.where(row >= want, row.astype(F32) - off, -1.0)
        a = jnp.zeros(rel.shape, F32)
        for r in range(wt_ref.shape[0]):
            a = jnp.where(rel == r, wt_ref[r:r + 1, :], a)
        a2 = a.astype(BF16)
        z_hi, z_lo = _unpack_bf16_pairs(buf[slot, k])

        def seg(zz):
            return lax.dot_general(a2, zz.astype(BF16), (((0,), (0,)), ((), ())), preferred_element_type=F32)
        o_ref[:, :half] += seg(z_hi)
        o_ref[:, half:] += seg(z_lo)
        return carry

    lax.fori_loop(0, n_chunks(i), chunk, 0)


def _combine(starts, x2, off, wt, z):
    n, d = x2.shape
    tb = COMBINE_TOKENS
    rc = COMBINE_ROWS
    max_chunks = (N_EXPERTS * tb + SUBLANES - 1 + rc - 1) // rc + 1
    assert n % tb == 0 and z.shape[0] % SUBLANES == 0 and z.shape[0] >= rc
    return pl.pallas_call(
        _combine_body,
        grid_spec=pltpu.PrefetchScalarGridSpec(
            num_scalar_prefetch=1,
            grid=(n // tb,),
            in_specs=[
                pl.BlockSpec((tb, d), lambda i, s: (i, 0)),
                pl.BlockSpec((1, tb), lambda i, s: (0, i)),
                pl.BlockSpec((N_EXPERTS, tb), lambda i, s: (0, i)),
                pl.BlockSpec(memory_space=pl.ANY),
            ],
            out_specs=pl.BlockSpec((tb, d), lambda i, s: (i, 0)),
            scratch_shapes=[pltpu.VMEM((2, max_chunks, rc, d // 2), U32), pltpu.SemaphoreType.DMA((2,))],
        ),
        out_shape=jax.ShapeDtypeStruct((n, d), F32),
        compiler_params=pltpu.CompilerParams(
            dimension_semantics=("arbitrary",), vmem_limit_bytes=VMEM_LIMIT),
        name="combine",
    )(starts, x2, off, wt, z)


def _layer(x, p):
    b, t, d = x.shape
    n = b * t
    head_dim = p["q_norm"].shape[-1]
    attn = N_HEADS * head_dim
    q, k, v, u = _in_proj(x.reshape(n, d), p["norm_mix"], p["w_in"], p["q_norm"], p["k_norm"],
                          attn=attn, head_dim=head_dim)
    shp = lambda a: a.reshape(b, t, a.shape[-1])
    x2, h2p, lg = _mix(shp(q), shp(k), shp(v), shp(u), x, p["bias_tabs"], p["w_pool"], p["pool_scale"],
                       p["w_out"], p["norm_ffn"], p["wr"], p["wr0"], head_dim=head_dim)

    nblk = n // LANES
    cap = max(1, (CAPACITY_FACTOR * n) // N_EXPERTS)
    lg = lg.reshape(n, LANES)[:, :N_EXPERTS]
    lg = lg.T.reshape(N_EXPERTS, nblk, LANES).transpose(0, 2, 1)
    idx, dst, wt, off = _route(lg, cap)

    tc = min(EXPERT_TOKENS, cap)
    tiles = lambda a: a.reshape(N_EXPERTS * (cap // tc), 1, tc)
    z = _experts(tiles(idx), tiles(dst), h2p.reshape(n, d // 2), p["w_gate"], p["w_up"], p["w_down"], cap)

    assert COMBINE_TOKENS == LANES
    starts = jnp.concatenate([off[0, :].astype(I32), jnp.full((1,), N_EXPERTS * cap, I32)])
    wt = wt.transpose(0, 2, 1).reshape(N_EXPERTS, n)
    out = _combine(starts, x2.reshape(n, d), off.T.reshape(1, n), wt, z)
    return out.reshape(b, t, d)


def _prepare(l, norm_mix, w_in, q_norm, k_norm, rel_pos_bias, w_pool, pool_scale, w_out, norm_ffn,
             w_router, w_gate, w_up, w_down):
    d = w_in.shape[1]
    whi = w_router[l].astype(BF16)
    wlo = (w_router[l] - whi.astype(F32)).astype(BF16)
    pad = lambda *cols: jnp.concatenate(
        list(cols) + [jnp.zeros((d, LANES - sum(c.shape[1] for c in cols)), BF16)], axis=1)
    return dict(
        norm_mix=norm_mix[l][None, :], w_in=w_in[l].astype(BF16),
        q_norm=q_norm[l][None, :], k_norm=k_norm[l][None, :],
        bias_tabs=_bias_tables(rel_pos_bias[l]), w_pool=w_pool[l].astype(BF16),
        pool_scale=pool_scale[l][None, :], w_out=w_out[l].astype(BF16), norm_ffn=norm_ffn[l][None, :],
        wr=pad(whi, wlo), wr0=pad(whi),
        w_gate=w_gate[l], w_up=w_up[l], w_down=w_down[l],
    )


def kernel(x_prompt, x_sample, norm_mix, w_in, q_norm, k_norm, rel_pos_bias, w_pool, pool_scale, w_out,
           norm_ffn, w_router, w_gate, w_up, w_down):
    weights = (norm_mix, w_in, q_norm, k_norm, rel_pos_bias, w_pool, pool_scale, w_out, norm_ffn,
               w_router, w_gate, w_up, w_down)
    layers = [_prepare(l, *weights) for l in range(w_in.shape[0])]

    def trunk(x):
        for p in layers:
            x = _layer(x, p)
        return x

    return trunk(x_prompt), trunk(x_sample)
```

```python
import functools

import numpy as np
import jax
import jax.numpy as jnp
from jax import lax
from jax.experimental import pallas as pl
from jax.experimental.pallas import tpu as pltpu

F32 = jnp.float32
BF16 = jnp.bfloat16
I32 = jnp.int32
U32 = jnp.uint32

GRID_W = 64
WIN_R = 8
WIN_C = 16
N_HEADS = 8
POOL_WINDOWS = (2, 4, 8, 16)
N_EXPERTS = 16
CAPACITY_FACTOR = 2
EPS = 1e-6

LANES = 128
SUBLANES = 8
MASKED = -1e30
VMEM_LIMIT = 56 * 1024 * 1024
HI16 = 0xFFFF0000
DIGIT = 64.0

ROWS_PER_STEP = 4
POOL_HALO = 8
IN_PROJ_TOKENS = 512
EXPERT_TOKENS = 1024
EXPERT_FF_STEPS = 11
ROW_UNROLL = 8
COMBINE_TOKENS = 128
COMBINE_ROWS = 128
COMBINE_SLOTS = 3


def _pack_bf16_pairs(x):
    b = lax.bitcast_convert_type(x, U32)
    h = x.shape[1] // 2
    return b[:, :h] | (b[:, h:] >> 16)


def _unpack_bf16_pairs(w):
    hi = lax.bitcast_convert_type(w & jnp.uint32(HI16), F32)
    lo = lax.bitcast_convert_type(w << 16, F32)
    return hi, lo


def _in_proj_body(x_ref, g_ref, w_ref, qn_ref, kn_ref, q_ref, k_ref, v_ref, u_ref, *, attn, head_dim):
    x = x_ref[...]
    h = x * lax.rsqrt(jnp.mean(x * x, axis=-1, keepdims=True) + EPS) * g_ref[...]
    h = h.astype(BF16)

    def head_norm(z, gain, scale, out_ref):
        for hd in range(attn // head_dim):
            sl = slice(hd * head_dim, (hd + 1) * head_dim)
            zh = z[:, sl]
            r = lax.rsqrt(jnp.mean(zh * zh, axis=-1, keepdims=True) + EPS)
            out_ref[:, sl] = (zh * r * gain * scale).astype(BF16)

    zq = jnp.dot(h, w_ref[:, 0:attn], preferred_element_type=F32)
    head_norm(zq, qn_ref[...], head_dim ** -0.5, q_ref)
    zk = jnp.dot(h, w_ref[:, attn:2 * attn], preferred_element_type=F32)
    head_norm(zk, kn_ref[...], 1.0, k_ref)
    v_ref[...] = jnp.dot(h, w_ref[:, 2 * attn:3 * attn], preferred_element_type=F32).astype(BF16)
    u_ref[...] = jnp.dot(h, w_ref[:, 3 * attn:], preferred_element_type=F32)


def _in_proj(x, gain, w_in, q_norm, k_norm, *, attn, head_dim):
    n, d = x.shape
    width = w_in.shape[1]
    pool = width - 3 * attn
    tm = min(IN_PROJ_TOKENS, n)
    const = lambda i: (0, 0)
    tok = lambda i: (i, 0)
    return pl.pallas_call(
        functools.partial(_in_proj_body, attn=attn, head_dim=head_dim),
        grid=(n // tm,),
        in_specs=[
            pl.BlockSpec((tm, d), tok),
            pl.BlockSpec((1, d), const),
            pl.BlockSpec((d, width), const, pipeline_mode=pl.Buffered(1)),
            pl.BlockSpec((1, head_dim), const),
            pl.BlockSpec((1, head_dim), const),
        ],
        out_specs=[
            pl.BlockSpec((tm, attn), tok),
            pl.BlockSpec((tm, attn), tok),
            pl.BlockSpec((tm, attn), tok),
            pl.BlockSpec((tm, pool), tok),
        ],
        out_shape=[
            jax.ShapeDtypeStruct((n, attn), BF16),
            jax.ShapeDtypeStruct((n, attn), BF16),
            jax.ShapeDtypeStruct((n, attn), BF16),
            jax.ShapeDtypeStruct((n, pool), F32),
        ],
        compiler_params=pltpu.CompilerParams(
            dimension_semantics=("parallel",), vmem_limit_bytes=VMEM_LIMIT),
        name="in_proj",
    )(x, gain, w_in, q_norm, k_norm)


def _bias_tables(rpb):
    rq = ROWS_PER_STEP
    nk = 3 * rq
    qi = np.arange(rq)[:, None]
    kj = np.arange(nk)[None, :]
    d = kj - rq - qi
    dr = np.clip(d + WIN_R - 1, 0, 2 * WIN_R - 2)
    ones = np.ones((rq, nk), bool)
    valid_first = ones & (kj >= rq) & (kj - rq < WIN_R)
    valid_mid = (d >= -(WIN_R // 2)) & (d < WIN_R // 2)
    valid_last = ones & (kj < 2 * rq)
    c = np.arange(GRID_W)[:, None]
    kc = np.arange(GRID_W)[None, :]
    cs = np.clip(c - WIN_C // 2, 0, GRID_W - WIN_C)
    valid_c = (kc >= cs) & (kc < cs + WIN_C)
    padded = jnp.pad(rpb, ((0, 0), (0, 0), (GRID_W, GRID_W)))
    first = GRID_W + WIN_C - 1
    by_col = jnp.stack([padded[:, :, first - ci:first - ci + GRID_W] for ci in range(GRID_W)],
                       axis=2)
    vals = jnp.stack([jnp.stack([by_col[:, dr[a, b]] for b in range(nk)], axis=2) for a in range(rq)],
                     axis=1)
    tabs = []
    for vr in (valid_first, valid_mid, valid_last):
        ok = vr[:, None, :, None] & valid_c[None, :, None, :]
        t = jnp.where(ok[None], vals, MASKED)
        tabs.append(t.reshape(rpb.shape[0], rq * GRID_W, nk * GRID_W))
    return jnp.stack(tabs).astype(F32)


def _mix_body(q_ref, kp_ref, kc_ref, kn_ref, vp_ref, vc_ref, vn_ref, bt_ref, uc_ref, up_ref, un_ref, x_ref,
              wpool_ref, pscale_ref, wout_ref, nffn_ref, wr_ref, wr0_ref,
              x2_ref, h2p_ref, lg_ref, mix_ref, *, seq_len, head_dim, n_heads, pool_ch):
    i = pl.program_id(1)
    nb = pl.num_programs(1)
    tq = q_ref.shape[0]
    attn = n_heads * head_dim

    kcat = jnp.concatenate([kp_ref[...], kc_ref[...], kn_ref[...]], axis=0)
    vcat = jnp.concatenate([vp_ref[...], vc_ref[...], vn_ref[...]], axis=0)
    for h in range(n_heads):
        sl = slice(h * head_dim, (h + 1) * head_dim)
        s = lax.dot_general(q_ref[:, sl], kcat[:, sl], (((1,), (1,)), ((), ())),
                            preferred_element_type=F32)
        s = s + bt_ref[h]
        m = jnp.max(s, axis=-1, keepdims=True)
        p = jnp.exp(s - m)
        l = jnp.sum(p, axis=-1, keepdims=True)
        o = jnp.dot(p.astype(BF16), vcat[:, sl], preferred_element_type=F32)
        mix_ref[:, sl] = (o / l).astype(BF16)

    ucur = uc_ref[...]
    up = jnp.where(i == 0, 0.0, up_ref[...])
    un = jnp.where(i == nb - 1, 0.0, un_ref[...])
    uext = jnp.concatenate([up, ucur, un], axis=0)
    t = i * tq + lax.broadcasted_iota(I32, (tq, 1), 0)
    for g, w in enumerate(POOL_WINDOWS):
        gs = slice(g * pool_ch, (g + 1) * pool_ch)
        ug = uext[:, gs]
        arr = ug[:-1] + ug[1:]
        sh = 2
        while sh < w:
            arr = arr[:-sh] + arr[sh:]
            sh *= 2
        start = POOL_HALO - w // 2
        wsum = arr[start:start + tq]
        cnt = (jnp.minimum(t + w // 2, seq_len) - jnp.maximum(t - w // 2, 0)).astype(F32)
        pg = (wsum / cnt - ucur[:, gs]).astype(BF16)
        yg = jnp.dot(pg, wpool_ref[g], preferred_element_type=F32) * pscale_ref[:, gs]
        mix_ref[:, attn + g * pool_ch:attn + (g + 1) * pool_ch] = yg.astype(BF16)

    x2 = x_ref[...] + jnp.dot(mix_ref[...], wout_ref[...], preferred_element_type=F32)
    x2_ref[...] = x2
    hn = x2 * lax.rsqrt(jnp.mean(x2 * x2, axis=-1, keepdims=True) + EPS) * nffn_ref[...]
    hi = hn.astype(BF16)
    hi32 = hi.astype(F32)
    lo = (hn - hi32).astype(BF16)
    r = (jnp.dot(hi, wr_ref[...], preferred_element_type=F32)
         + jnp.dot(lo, wr0_ref[...], preferred_element_type=F32))
    lg_ref[...] = r + pltpu.roll(r, LANES - N_EXPERTS, 1)
    h2p_ref[...] = _pack_bf16_pairs(hi32)


def _mix(q, k, v, u, x, bias_tabs, w_pool, pool_scale, w_out, norm_ffn, wr, wr0, *, head_dim):
    b, t, attn = q.shape
    d = x.shape[-1]
    pool = u.shape[-1]
    n_heads = attn // head_dim
    pool_ch = pool // len(POOL_WINDOWS)
    tq = ROWS_PER_STEP * GRID_W
    nb = t // tq
    assert 2 * ROWS_PER_STEP == WIN_R and t % tq == 0 and nb >= 2
    hb = tq // POOL_HALO

    cur = lambda bi, i: (bi, i, 0)
    prev = lambda bi, i: (bi, jnp.maximum(i - 1, 0), 0)
    nxt = lambda bi, i: (bi, jnp.minimum(i + 1, nb - 1), 0)
    c2 = lambda bi, i: (0, 0)
    c3 = lambda bi, i: (0, 0, 0)
    kv = lambda im: pl.BlockSpec((None, tq, attn), im)
    return pl.pallas_call(
        functools.partial(_mix_body, seq_len=t, head_dim=head_dim, n_heads=n_heads, pool_ch=pool_ch),
        grid=(b, nb),
        in_specs=[
            kv(cur), kv(prev), kv(cur), kv(nxt), kv(prev), kv(cur), kv(nxt),
            pl.BlockSpec((None, n_heads, tq, 3 * tq),
                         lambda bi, i: (jnp.where(i == 0, 0, jnp.where(i == nb - 1, 2, 1)), 0, 0, 0)),
            pl.BlockSpec((None, tq, pool), cur),
            pl.BlockSpec((None, POOL_HALO, pool), lambda bi, i: (bi, jnp.maximum(i * hb - 1, 0), 0)),
            pl.BlockSpec((None, POOL_HALO, pool),
                         lambda bi, i: (bi, jnp.minimum((i + 1) * hb, t // POOL_HALO - 1), 0)),
            pl.BlockSpec((None, tq, d), cur),
            pl.BlockSpec(w_pool.shape, c3),
            pl.BlockSpec((1, pool), c2),
            pl.BlockSpec(w_out.shape, c2, pipeline_mode=pl.Buffered(1)),
            pl.BlockSpec((1, d), c2),
            pl.BlockSpec(wr.shape, c2),
            pl.BlockSpec(wr0.shape, c2),
        ],
        out_specs=[
            pl.BlockSpec((None, tq, d), cur),
            pl.BlockSpec((None, tq, d // 2), cur),
            pl.BlockSpec((None, tq, LANES), cur),
        ],
        out_shape=[
            jax.ShapeDtypeStruct((b, t, d), F32),
            jax.ShapeDtypeStruct((b, t, d // 2), U32),
            jax.ShapeDtypeStruct((b, t, LANES), F32),
        ],
        scratch_shapes=[pltpu.VMEM((tq, attn + pool), BF16)],
        compiler_params=pltpu.CompilerParams(
            dimension_semantics=("parallel", "arbitrary"), vmem_limit_bytes=VMEM_LIMIT),
        name="mix",
    )(q, k, k, k, v, v, v, bias_tabs, u, u, u, x, w_pool, pool_scale, w_out, norm_ffn, wr, wr0)


def _route_body(lg_ref, idx_ref, dst_ref, wt_ref, off_ref, aff_ref, sel_ref, rank_ref, thr_ref, *, cap):
    n_exp, tl, nblk = lg_ref.shape

    m = lg_ref[0]
    for e in range(1, n_exp):
        m = jnp.maximum(m, lg_ref[e])
    den = jnp.zeros_like(m)
    for e in range(n_exp):
        ex = jnp.exp(lg_ref[e] - m)
        aff_ref[e] = ex
        den = den + ex
    for e in range(n_exp):
        aff_ref[e] = aff_ref[e] / den

    def total(x):
        return jnp.sum(jnp.sum(x, axis=0, keepdims=True), axis=1, keepdims=True)

    def search(it, thr):
        bit = jnp.left_shift(jnp.int32(1), 30 - it)
        out = []
        for e in range(n_exp):
            bits = lax.bitcast_convert_type(aff_ref[e], I32)
            cand = thr[e] | bit
            c = total(jnp.where(bits >= cand, 1.0, 0.0))
            out.append(jnp.where(c >= cap, cand, thr[e]))
        return tuple(out)

    thr = lax.fori_loop(0, 31, search, tuple(jnp.zeros((1, 1), I32) for _ in range(n_exp)))
    for e in range(n_exp):
        thr_ref[e] = jnp.broadcast_to(thr[e], (SUBLANES, LANES))

    def tri():
        r_i = lax.broadcasted_iota(I32, (tl, tl), 0)
        c_i = lax.broadcasted_iota(I32, (tl, tl), 1)
        return jnp.where(c_i <= r_i, 1.0, 0.0).astype(BF16)

    def before(col_form=False):
        rb = lax.broadcasted_iota(I32, (nblk, nblk), 0)
        cb = lax.broadcasted_iota(I32, (nblk, nblk), 1)
        return jnp.where((cb < rb) if col_form else (rb < cb), 1.0, 0.0).astype(BF16)

    def prefix(mask):
        incl = jnp.dot(tri(), mask.astype(BF16), preferred_element_type=F32)
        tot8 = jnp.broadcast_to(incl[tl - 1:tl, :], (SUBLANES, nblk)).astype(BF16)
        base = jnp.dot(tot8, before(), preferred_element_type=F32)[0:1, :]
        return incl, base

    for r in range(n_exp):
        wt_ref[r] = jnp.zeros((tl, nblk), F32)

    def select(e, rank):
        aff = aff_ref[e]
        bits = lax.bitcast_convert_type(aff, I32)
        thr_e = thr_ref[e][0:1, 0:1]
        gt = bits > thr_e
        eq = bits == thr_e
        need = cap - total(jnp.where(gt, 1.0, 0.0))
        eqf = jnp.where(eq, 1.0, 0.0)
        incl_eq, base_eq = prefix(eqf)
        eq_rank = base_eq + incl_eq - eqf
        sel = gt | (eq & (eq_rank < need))
        self_ = jnp.where(sel, 1.0, 0.0)
        for r in range(n_exp):
            wt_ref[r] = jnp.where(sel & (rank == r), aff, wt_ref[r])
        sel_ref[e] = self_
        rank_ref[e] = rank
        return rank + self_

    rank = lax.fori_loop(0, n_exp, select, jnp.zeros((tl, nblk), F32))

    incl_n = jnp.dot(tri(), rank.astype(BF16), preferred_element_type=F32)
    tot_n = jnp.broadcast_to(incl_n[tl - 1:tl, :], (SUBLANES, nblk))
    tot_hi = jnp.floor(tot_n * (1.0 / DIGIT))
    tot_lo = tot_n - tot_hi * DIGIT
    base_n = (DIGIT * jnp.dot(tot_hi.astype(BF16), before(), preferred_element_type=F32)
              + jnp.dot(tot_lo.astype(BF16), before(), preferred_element_type=F32))[0:1, :]
    off_ref[...] = base_n + incl_n - rank

    def place(e, carry):
        slot_f = lax.broadcasted_iota(I32, (nblk, cap), 1).astype(F32)
        blk_f = lax.broadcasted_iota(I32, (nblk, cap), 0).astype(F32)
        tl_f = lax.broadcasted_iota(I32, (tl, cap), 0).astype(F32)
        self_ = sel_ref[e]
        incl, _ = prefix(self_)
        tot_c = jnp.sum(self_.T, axis=1, keepdims=True)
        base_c = jnp.dot(before(col_form=True), jnp.broadcast_to(tot_c, (nblk, LANES)).astype(BF16),
                         preferred_element_type=F32)[:, 0:1]
        owns = (base_c <= slot_f) & (slot_f < base_c + tot_c)
        local = jnp.sum(jnp.where(owns, slot_f - base_c, 0.0), axis=0, keepdims=True)
        blk = jnp.sum(jnp.where(owns, blk_f, 0.0), axis=0, keepdims=True)
        owns_b = jnp.where(owns, 1.0, 0.0).astype(BF16)
        incl_at = jnp.dot(incl.astype(BF16), owns_b, preferred_element_type=F32)
        pos = jnp.sum(jnp.where(incl_at <= local, 1.0, 0.0), axis=0, keepdims=True)
        idx_ref[e] = (blk * tl + pos).astype(I32)

        row = off_ref[...] + rank_ref[e]
        d2 = jnp.floor(row * (1.0 / (DIGIT * DIGIT)))
        rem = row - d2 * (DIGIT * DIGIT)
        d1 = jnp.floor(rem * (1.0 / DIGIT))
        d0 = rem - d1 * DIGIT
        at = lambda dg: jnp.dot(dg.astype(BF16), owns_b, preferred_element_type=F32)
        row_at = (DIGIT * DIGIT) * at(d2) + DIGIT * at(d1) + at(d0)
        dst_ref[e] = jnp.sum(jnp.where(tl_f == pos, row_at, 0.0), axis=0, keepdims=True).astype(I32)
        return carry

    lax.fori_loop(0, n_exp, place, 0)


def _route(lg, cap):
    n_exp, tl, nblk = lg.shape
    full = lambda: pl.BlockSpec(memory_space=pltpu.VMEM)
    return pl.pallas_call(
        functools.partial(_route_body, cap=cap),
        in_specs=[full()],
        out_specs=[full() for _ in range(4)],
        out_shape=[
            jax.ShapeDtypeStruct((n_exp, 1, cap), I32),
            jax.ShapeDtypeStruct((n_exp, 1, cap), I32),
            jax.ShapeDtypeStruct((n_exp, tl, nblk), F32),
            jax.ShapeDtypeStruct((tl, nblk), F32),
        ],
        scratch_shapes=[pltpu.VMEM((n_exp, tl, nblk), F32) for _ in range(3)]
                       + [pltpu.VMEM((n_exp, SUBLANES, LANES), I32)],
        compiler_params=pltpu.CompilerParams(vmem_limit_bytes=VMEM_LIMIT),
        name="route",
    )(lg)


def _expert_body(idx_ref, nxt_ref, prev_dst_ref, dst_ref, h_ref, wg_ref, wu_ref, wd_ref, z_ref,
                 rows, x_scr, acc_ref, y_scr, sem, out_sem, *, tc):
    e, c, f = pl.program_id(0), pl.program_id(1), pl.program_id(2)
    tile = e * pl.num_programs(1) + c
    n_tiles = pl.num_programs(0) * pl.num_programs(1)
    last_f = pl.num_programs(2) - 1
    padded, half = rows.shape
    per_step = padded // EXPERT_FF_STEPS

    def gather_row(tok_ref, r):
        return pltpu.make_async_copy(h_ref.at[pl.ds(tok_ref[0, r], 1)], rows.at[pl.ds(r, 1)], sem)

    def scatter_row(row_ref, r):
        return pltpu.make_async_copy(y_scr.at[pl.ds(r, 1)], z_ref.at[pl.ds(row_ref[0, r], 1)], out_sem)

    def all_gathered():
        return pltpu.make_async_copy(h_ref.at[pl.ds(0, padded)], rows, sem)

    def all_scattered(n_rows):
        return pltpu.make_async_copy(y_scr.at[pl.ds(0, n_rows)], z_ref.at[pl.ds(0, n_rows)], out_sem)

    def in_groups(n_rows, start_row):
        def group(g, carry):
            first = pl.multiple_of(g * ROW_UNROLL, ROW_UNROLL)
            for k in range(ROW_UNROLL):
                start_row(first + k)
            return carry
        lax.fori_loop(0, n_rows // ROW_UNROLL, group, 0)

    @pl.when((tile == 0) & (f == 0))
    def _():
        y_scr[...] = jnp.zeros_like(y_scr)
        in_groups(padded, lambda r: gather_row(idx_ref, r).start())

    @pl.when(f == 0)
    def _():
        all_gathered().wait()
        hi, lo = _unpack_bf16_pairs(rows[0:tc, :])
        x_scr[:, :half] = hi.astype(BF16)
        x_scr[:, half:] = lo.astype(BF16)
        acc_ref[...] = jnp.zeros_like(acc_ref)

    for g in range(per_step // ROW_UNROLL):
        first = pl.multiple_of(f * per_step + g * ROW_UNROLL, ROW_UNROLL)
        for k in range(ROW_UNROLL):
            gather_row(nxt_ref, first + k).start()
            scatter_row(prev_dst_ref, first + k).start()

    x = x_scr[...]
    a = jnp.dot(x, wg_ref[...].astype(BF16), preferred_element_type=F32)
    b = jnp.dot(x, wu_ref[...].astype(BF16), preferred_element_type=F32)
    hmid = (a * jax.nn.sigmoid(a) * b).astype(BF16)
    acc_ref[...] += jnp.dot(hmid, wd_ref[...].astype(BF16), preferred_element_type=F32)

    @pl.when(f == last_f)
    def _():
        all_scattered(padded).wait()
        y_scr[0:tc, :] = _pack_bf16_pairs(acc_ref[...].astype(BF16).astype(F32))

        @pl.when(tile == n_tiles - 1)
        def _():
            in_groups(tc, lambda r: scatter_row(dst_ref, r).start())
            all_scattered(tc).wait()
            all_gathered().wait()


def _experts(idx, dst, h2p, w_gate, w_up, w_down, cap):
    n_exp, d, ff = w_gate.shape
    half = h2p.shape[1]
    tc = idx.shape[-1]
    tf = ff // EXPERT_FF_STEPS
    ct = cap // tc
    n_tiles = n_exp * ct
    assert cap % tc == 0 and ff % EXPERT_FF_STEPS == 0 and tf % LANES == 0 and tc % ROW_UNROLL == 0
    per_step = -(-tc // (EXPERT_FF_STEPS * ROW_UNROLL)) * ROW_UNROLL
    padded = per_step * EXPERT_FF_STEPS
    spare = padded - tc
    idx = jnp.pad(idx, ((0, 0), (0, 0), (0, spare)))
    dst = jnp.concatenate(
        [dst, jnp.broadcast_to(n_exp * cap + jnp.arange(spare, dtype=I32), (n_tiles, 1, spare))], axis=2)
    smem = lambda im: pl.BlockSpec((None, 1, padded), im, memory_space=pltpu.SMEM)
    return pl.pallas_call(
        functools.partial(_expert_body, tc=tc),
        grid=(n_exp, ct, EXPERT_FF_STEPS),
        in_specs=[
            smem(lambda e, c, f: (e * ct + c, 0, 0)),
            smem(lambda e, c, f: (jnp.minimum(e * ct + c + 1, n_tiles - 1), 0, 0)),
            smem(lambda e, c, f: (jnp.maximum(e * ct + c - 1, 0), 0, 0)),
            smem(lambda e, c, f: (e * ct + c, 0, 0)),
            pl.BlockSpec(memory_space=pl.ANY),
            pl.BlockSpec((None, d, tf), lambda e, c, f: (e, 0, f)),
            pl.BlockSpec((None, d, tf), lambda e, c, f: (e, 0, f)),
            pl.BlockSpec((None, tf, d), lambda e, c, f: (e, f, 0)),
        ],
        out_specs=pl.BlockSpec(memory_space=pl.ANY),
        out_shape=jax.ShapeDtypeStruct((n_exp * cap + max(spare, SUBLANES), half), U32),
        scratch_shapes=[pltpu.VMEM((padded, half), U32), pltpu.VMEM((tc, d), BF16), pltpu.VMEM((tc, d), F32),
                        pltpu.VMEM((padded, half), U32), pltpu.SemaphoreType.DMA(()),
                        pltpu.SemaphoreType.DMA(())],
        compiler_params=pltpu.CompilerParams(
            dimension_semantics=("arbitrary", "arbitrary", "arbitrary"), vmem_limit_bytes=VMEM_LIMIT,
            has_side_effects=True),
        name="experts",
    )(idx, idx, dst, dst, h2p, w_gate, w_up, w_down)


def _combine_body(start_ref, x2_ref, off_ref, wt_ref, z_ref, o_ref, buf, sem):
    i = pl.program_id(0)
    nsteps = pl.num_programs(0)
    _, _, rc, half = buf.shape
    n_rows = z_ref.shape[0]

    def first_row(b):
        return (start_ref[b] // SUBLANES) * SUBLANES

    def n_chunks(b):
        return (start_ref[b + 1] - first_row(b) + rc - 1) // rc

    def chunk_copy(b, k, slot):
        row = pl.multiple_of(jnp.minimum(first_row(b) + k * rc, n_rows - rc), SUBLANES)
        return pltpu.make_async_copy(z_ref.at[pl.ds(row, rc)], buf.at[slot, k], sem.at[slot])

    def fetch(b, slot):
        def one(k, carry):
            chunk_copy(b, k, slot).start()
            return carry
        lax.fori_loop(0, n_chunks(b), one, 0)

    n_slots = buf.shape[0]

    @pl.when(i == 0)
    def _():
        fetch(0, 0)
        fetch(1, 1)

    @pl.when(i + 2 < nsteps)
    def _():
        fetch(i + 2, (i + 2) % n_slots)

    slot = i % n_slots
    o_ref[...] = x2_ref[...]
    off = off_ref[...]
    sub = lax.broadcasted_iota(I32, (rc, 1), 0)

    def landed(k, carry):
        chunk_copy(i, k, slot).wait()
        return carry

    lax.fori_loop(0, n_chunks(i), landed, 0)

    def chunk(k, carry):
        want = first_row(i) + k * rc
        row = jnp.minimum(want, n_rows - rc) + sub
        rel = jnp.where(row >= want, row.astype(F32) - off, -1.0)
        a = jnp.zeros(rel.shape, F32)
        for r in range(wt_ref.shape[0]):
            a = jnp.where(rel == r, wt_ref[r:r + 1, :], a)
        a2 = a.astype(BF16)
        z_hi, z_lo = _unpack_bf16_pairs(buf[slot, k])

        def seg(zz):
            return lax.dot_general(a2, zz.astype(BF16), (((0,), (0,)), ((), ())), preferred_element_type=F32)
        o_ref[:, :half] += seg(z_hi)
        o_ref[:, half:] += seg(z_lo)
        return carry

    lax.fori_loop(0, n_chunks(i), chunk, 0)


def _combine(starts, x2, off, wt, z):
    n, d = x2.shape
    tb = COMBINE_TOKENS
    rc = COMBINE_ROWS
    max_chunks = (N_EXPERTS * tb + SUBLANES - 1 + rc - 1) // rc + 1
    assert n % tb == 0 and n // tb >= 2 and z.shape[0] % SUBLANES == 0 and z.shape[0] >= rc
    return pl.pallas_call(
        _combine_body,
        grid_spec=pltpu.PrefetchScalarGridSpec(
            num_scalar_prefetch=1,
            grid=(n // tb,),
            in_specs=[
                pl.BlockSpec((tb, d), lambda i, s: (i, 0)),
                pl.BlockSpec((1, tb), lambda i, s: (0, i)),
                pl.BlockSpec((N_EXPERTS, tb), lambda i, s: (0, i)),
                pl.BlockSpec(memory_space=pl.ANY),
            ],
            out_specs=pl.BlockSpec((tb, d), lambda i, s: (i, 0)),
            scratch_shapes=[pltpu.VMEM((COMBINE_SLOTS, max_chunks, rc, d // 2), U32),
                            pltpu.SemaphoreType.DMA((COMBINE_SLOTS,))],
        ),
        out_shape=jax.ShapeDtypeStruct((n, d), F32),
        compiler_params=pltpu.CompilerParams(
            dimension_semantics=("arbitrary",), vmem_limit_bytes=VMEM_LIMIT),
        name="combine",
    )(starts, x2, off, wt, z)


def _layer(x, p):
    b, t, d = x.shape
    n = b * t
    head_dim = p["q_norm"].shape[-1]
    attn = N_HEADS * head_dim
    q, k, v, u = _in_proj(x.reshape(n, d), p["norm_mix"], p["w_in"], p["q_norm"], p["k_norm"],
                          attn=attn, head_dim=head_dim)
    shp = lambda a: a.reshape(b, t, a.shape[-1])
    x2, h2p, lg = _mix(shp(q), shp(k), shp(v), shp(u), x, p["bias_tabs"], p["w_pool"], p["pool_scale"],
                       p["w_out"], p["norm_ffn"], p["wr"], p["wr0"], head_dim=head_dim)

    nblk = n // LANES
    cap = max(1, (CAPACITY_FACTOR * n) // N_EXPERTS)
    lg = lg.reshape(n, LANES)[:, :N_EXPERTS]
    lg = lg.T.reshape(N_EXPERTS, nblk, LANES).transpose(0, 2, 1)
    idx, dst, wt, off = _route(lg, cap)

    tc = min(EXPERT_TOKENS, cap)
    tiles = lambda a: a.reshape(N_EXPERTS * (cap // tc), 1, tc)
    z = _experts(tiles(idx), tiles(dst), h2p.reshape(n, d // 2), p["w_gate"], p["w_up"], p["w_down"], cap)

    assert COMBINE_TOKENS == LANES
    starts = jnp.concatenate([off[0, :].astype(I32), jnp.full((1,), N_EXPERTS * cap, I32)])
    wt = wt.transpose(0, 2, 1).reshape(N_EXPERTS, n)
    out = _combine(starts, x2.reshape(n, d), off.T.reshape(1, n), wt, z)
    return out.reshape(b, t, d)


def _prepare(l, norm_mix, w_in, q_norm, k_norm, rel_pos_bias, w_pool, pool_scale, w_out, norm_ffn,
             w_router, w_gate, w_up, w_down):
    d = w_in.shape[1]
    whi = w_router[l].astype(BF16)
    wlo = (w_router[l] - whi.astype(F32)).astype(BF16)
    pad = lambda *cols: jnp.concatenate(
        list(cols) + [jnp.zeros((d, LANES - sum(c.shape[1] for c in cols)), BF16)], axis=1)
    return dict(
        norm_mix=norm_mix[l][None, :], w_in=w_in[l].astype(BF16),
        q_norm=q_norm[l][None, :], k_norm=k_norm[l][None, :],
        bias_tabs=_bias_tables(rel_pos_bias[l]), w_pool=w_pool[l].astype(BF16),
        pool_scale=pool_scale[l][None, :], w_out=w_out[l].astype(BF16), norm_ffn=norm_ffn[l][None, :],
        wr=pad(whi, wlo), wr0=pad(whi),
        w_gate=w_gate[l], w_up=w_up[l], w_down=w_down[l],
    )


def kernel(x_prompt, x_sample, norm_mix, w_in, q_norm, k_norm, rel_pos_bias, w_pool, pool_scale, w_out,
           norm_ffn, w_router, w_gate, w_up, w_down):
    weights = (norm_mix, w_in, q_norm, k_norm, rel_pos_bias, w_pool, pool_scale, w_out, norm_ffn,
               w_router, w_gate, w_up, w_down)
    layers = [_prepare(l, *weights) for l in range(w_in.shape[0])]

    def trunk(x):
        for p in layers:
            x = _layer(x, p)
        return x

    return trunk(x_prompt), trunk(x_sample)
```

```python
import functools

import numpy as np
import jax
import jax.numpy as jnp
from jax import lax
from jax.experimental import pallas as pl
from jax.experimental.pallas import tpu as pltpu

F32 = jnp.float32
BF16 = jnp.bfloat16
I32 = jnp.int32
U32 = jnp.uint32

GRID_W = 64
WIN_R = 8
WIN_C = 16
N_HEADS = 8
POOL_WINDOWS = (2, 4, 8, 16)
N_EXPERTS = 16
CAPACITY_FACTOR = 2
EPS = 1e-6

LANES = 128
SUBLANES = 8
MASKED = -1e30
VMEM_LIMIT = 56 * 1024 * 1024
HI16 = 0xFFFF0000
DIGIT = 64.0

ROWS_PER_STEP = 4
POOL_HALO = 8
IN_PROJ_TOKENS = 512
EXPERT_TOKENS = 1024
EXPERT_FF_STEPS = 11
ROW_UNROLL = 8
ROW_DMA_QUEUE = 1
COMBINE_TOKENS = 128
COMBINE_ROWS = 128
COMBINE_SLOTS = 3


def _pack_bf16_pairs(x):
    b = lax.bitcast_convert_type(x, U32)
    h = x.shape[1] // 2
    return b[:, :h] | (b[:, h:] >> 16)


def _unpack_bf16_pairs(w):
    hi = lax.bitcast_convert_type(w & jnp.uint32(HI16), F32)
    lo = lax.bitcast_convert_type(w << 16, F32)
    return hi, lo


def _in_proj_body(x_ref, g_ref, w_ref, qn_ref, kn_ref, q_ref, k_ref, v_ref, u_ref, *, attn, head_dim):
    x = x_ref[...]
    h = x * lax.rsqrt(jnp.mean(x * x, axis=-1, keepdims=True) + EPS) * g_ref[...]
    h = h.astype(BF16)

    def head_norm(z, gain, scale, out_ref):
        for hd in range(attn // head_dim):
            sl = slice(hd * head_dim, (hd + 1) * head_dim)
            zh = z[:, sl]
            r = lax.rsqrt(jnp.mean(zh * zh, axis=-1, keepdims=True) + EPS)
            out_ref[:, sl] = (zh * r * gain * scale).astype(BF16)

    zq = jnp.dot(h, w_ref[:, 0:attn], preferred_element_type=F32)
    head_norm(zq, qn_ref[...], head_dim ** -0.5, q_ref)
    zk = jnp.dot(h, w_ref[:, attn:2 * attn], preferred_element_type=F32)
    head_norm(zk, kn_ref[...], 1.0, k_ref)
    v_ref[...] = jnp.dot(h, w_ref[:, 2 * attn:3 * attn], preferred_element_type=F32).astype(BF16)
    u_ref[...] = jnp.dot(h, w_ref[:, 3 * attn:], preferred_element_type=F32)


def _in_proj(x, gain, w_in, q_norm, k_norm, *, attn, head_dim):
    n, d = x.shape
    width = w_in.shape[1]
    pool = width - 3 * attn
    tm = min(IN_PROJ_TOKENS, n)
    const = lambda i: (0, 0)
    tok = lambda i: (i, 0)
    return pl.pallas_call(
        functools.partial(_in_proj_body, attn=attn, head_dim=head_dim),
        grid=(n // tm,),
        in_specs=[
            pl.BlockSpec((tm, d), tok),
            pl.BlockSpec((1, d), const),
            pl.BlockSpec((d, width), const, pipeline_mode=pl.Buffered(1)),
            pl.BlockSpec((1, head_dim), const),
            pl.BlockSpec((1, head_dim), const),
        ],
        out_specs=[
            pl.BlockSpec((tm, attn), tok),
            pl.BlockSpec((tm, attn), tok),
            pl.BlockSpec((tm, attn), tok),
            pl.BlockSpec((tm, pool), tok),
        ],
        out_shape=[
            jax.ShapeDtypeStruct((n, attn), BF16),
            jax.ShapeDtypeStruct((n, attn), BF16),
            jax.ShapeDtypeStruct((n, attn), BF16),
            jax.ShapeDtypeStruct((n, pool), F32),
        ],
        compiler_params=pltpu.CompilerParams(
            dimension_semantics=("parallel",), vmem_limit_bytes=VMEM_LIMIT),
        name="in_proj",
    )(x, gain, w_in, q_norm, k_norm)


def _bias_tables(rpb):
    rq = ROWS_PER_STEP
    nk = 3 * rq
    qi = np.arange(rq)[:, None]
    kj = np.arange(nk)[None, :]
    d = kj - rq - qi
    dr = np.clip(d + WIN_R - 1, 0, 2 * WIN_R - 2)
    ones = np.ones((rq, nk), bool)
    valid_first = ones & (kj >= rq) & (kj - rq < WIN_R)
    valid_mid = (d >= -(WIN_R // 2)) & (d < WIN_R // 2)
    valid_last = ones & (kj < 2 * rq)
    c = np.arange(GRID_W)[:, None]
    kc = np.arange(GRID_W)[None, :]
    cs = np.clip(c - WIN_C // 2, 0, GRID_W - WIN_C)
    valid_c = (kc >= cs) & (kc < cs + WIN_C)
    padded = jnp.pad(rpb, ((0, 0), (0, 0), (GRID_W, GRID_W)))
    first = GRID_W + WIN_C - 1
    by_col = jnp.stack([padded[:, :, first - ci:first - ci + GRID_W] for ci in range(GRID_W)],
                       axis=2)
    vals = jnp.stack([jnp.stack([by_col[:, dr[a, b]] for b in range(nk)], axis=2) for a in range(rq)],
                     axis=1)
    tabs = []
    for vr in (valid_first, valid_mid, valid_last):
        ok = vr[:, None, :, None] & valid_c[None, :, None, :]
        t = jnp.where(ok[None], vals, MASKED)
        tabs.append(t.reshape(rpb.shape[0], rq * GRID_W, nk * GRID_W))
    return jnp.stack(tabs).astype(F32)


def _mix_body(q_ref, kp_ref, kc_ref, kn_ref, vp_ref, vc_ref, vn_ref, bt_ref, uc_ref, up_ref, un_ref, x_ref,
              wpool_ref, pscale_ref, wout_ref, nffn_ref, wr_ref, wr0_ref,
              x2_ref, h2p_ref, lg_ref, mix_ref, *, seq_len, head_dim, n_heads, pool_ch):
    i = pl.program_id(1)
    nb = pl.num_programs(1)
    tq = q_ref.shape[0]
    attn = n_heads * head_dim

    kcat = jnp.concatenate([kp_ref[...], kc_ref[...], kn_ref[...]], axis=0)
    vcat = jnp.concatenate([vp_ref[...], vc_ref[...], vn_ref[...]], axis=0)
    for h in range(n_heads):
        sl = slice(h * head_dim, (h + 1) * head_dim)
        s = lax.dot_general(q_ref[:, sl], kcat[:, sl], (((1,), (1,)), ((), ())),
                            preferred_element_type=F32)
        s = s + bt_ref[h]
        m = jnp.max(s, axis=-1, keepdims=True)
        p = jnp.exp(s - m)
        l = jnp.sum(p, axis=-1, keepdims=True)
        o = jnp.dot(p.astype(BF16), vcat[:, sl], preferred_element_type=F32)
        mix_ref[:, sl] = (o / l).astype(BF16)

    ucur = uc_ref[...]
    up = jnp.where(i == 0, 0.0, up_ref[...])
    un = jnp.where(i == nb - 1, 0.0, un_ref[...])
    uext = jnp.concatenate([up, ucur, un], axis=0)
    t = i * tq + lax.broadcasted_iota(I32, (tq, 1), 0)
    for g, w in enumerate(POOL_WINDOWS):
        gs = slice(g * pool_ch, (g + 1) * pool_ch)
        ug = uext[:, gs]
        arr = ug[:-1] + ug[1:]
        sh = 2
        while sh < w:
            arr = arr[:-sh] + arr[sh:]
            sh *= 2
        start = POOL_HALO - w // 2
        wsum = arr[start:start + tq]
        cnt = (jnp.minimum(t + w // 2, seq_len) - jnp.maximum(t - w // 2, 0)).astype(F32)
        pg = (wsum / cnt - ucur[:, gs]).astype(BF16)
        yg = jnp.dot(pg, wpool_ref[g], preferred_element_type=F32) * pscale_ref[:, gs]
        mix_ref[:, attn + g * pool_ch:attn + (g + 1) * pool_ch] = yg.astype(BF16)

    x2 = x_ref[...] + jnp.dot(mix_ref[...], wout_ref[...], preferred_element_type=F32)
    x2_ref[...] = x2
    hn = x2 * lax.rsqrt(jnp.mean(x2 * x2, axis=-1, keepdims=True) + EPS) * nffn_ref[...]
    hi = hn.astype(BF16)
    hi32 = hi.astype(F32)
    lo = (hn - hi32).astype(BF16)
    r = (jnp.dot(hi, wr_ref[...], preferred_element_type=F32)
         + jnp.dot(lo, wr0_ref[...], preferred_element_type=F32))
    lg_ref[...] = r + pltpu.roll(r, LANES - N_EXPERTS, 1)
    h2p_ref[...] = _pack_bf16_pairs(hi32)


def _mix(q, k, v, u, x, bias_tabs, w_pool, pool_scale, w_out, norm_ffn, wr, wr0, *, head_dim):
    b, t, attn = q.shape
    d = x.shape[-1]
    pool = u.shape[-1]
    n_heads = attn // head_dim
    pool_ch = pool // len(POOL_WINDOWS)
    tq = ROWS_PER_STEP * GRID_W
    nb = t // tq
    assert 2 * ROWS_PER_STEP == WIN_R and t % tq == 0 and nb >= 2
    hb = tq // POOL_HALO

    cur = lambda bi, i: (bi, i, 0)
    prev = lambda bi, i: (bi, jnp.maximum(i - 1, 0), 0)
    nxt = lambda bi, i: (bi, jnp.minimum(i + 1, nb - 1), 0)
    c2 = lambda bi, i: (0, 0)
    c3 = lambda bi, i: (0, 0, 0)
    kv = lambda im: pl.BlockSpec((None, tq, attn), im)
    return pl.pallas_call(
        functools.partial(_mix_body, seq_len=t, head_dim=head_dim, n_heads=n_heads, pool_ch=pool_ch),
        grid=(b, nb),
        in_specs=[
            kv(cur), kv(prev), kv(cur), kv(nxt), kv(prev), kv(cur), kv(nxt),
            pl.BlockSpec((None, n_heads, tq, 3 * tq),
                         lambda bi, i: (jnp.where(i == 0, 0, jnp.where(i == nb - 1, 2, 1)), 0, 0, 0)),
            pl.BlockSpec((None, tq, pool), cur),
            pl.BlockSpec((None, POOL_HALO, pool), lambda bi, i: (bi, jnp.maximum(i * hb - 1, 0), 0)),
            pl.BlockSpec((None, POOL_HALO, pool),
                         lambda bi, i: (bi, jnp.minimum((i + 1) * hb, t // POOL_HALO - 1), 0)),
            pl.BlockSpec((None, tq, d), cur),
            pl.BlockSpec(w_pool.shape, c3),
            pl.BlockSpec((1, pool), c2),
            pl.BlockSpec(w_out.shape, c2, pipeline_mode=pl.Buffered(1)),
            pl.BlockSpec((1, d), c2),
            pl.BlockSpec(wr.shape, c2),
            pl.BlockSpec(wr0.shape, c2),
        ],
        out_specs=[
            pl.BlockSpec((None, tq, d), cur),
            pl.BlockSpec((None, tq, d // 2), cur),
            pl.BlockSpec((None, tq, LANES), cur),
        ],
        out_shape=[
            jax.ShapeDtypeStruct((b, t, d), F32),
            jax.ShapeDtypeStruct((b, t, d // 2), U32),
            jax.ShapeDtypeStruct((b, t, LANES), F32),
        ],
        scratch_shapes=[pltpu.VMEM((tq, attn + pool), BF16)],
        compiler_params=pltpu.CompilerParams(
            dimension_semantics=("parallel", "arbitrary"), vmem_limit_bytes=VMEM_LIMIT),
        name="mix",
    )(q, k, k, k, v, v, v, bias_tabs, u, u, u, x, w_pool, pool_scale, w_out, norm_ffn, wr, wr0)


def _route_body(lg_ref, idx_ref, dst_ref, wt_ref, off_ref, aff_ref, sel_ref, rank_ref, thr_ref, *, cap):
    n_exp, tl, nblk = lg_ref.shape

    m = lg_ref[0]
    for e in range(1, n_exp):
        m = jnp.maximum(m, lg_ref[e])
    den = jnp.zeros_like(m)
    for e in range(n_exp):
        ex = jnp.exp(lg_ref[e] - m)
        aff_ref[e] = ex
        den = den + ex
    for e in range(n_exp):
        aff_ref[e] = aff_ref[e] / den

    def total(x):
        return jnp.sum(jnp.sum(x, axis=0, keepdims=True), axis=1, keepdims=True)

    def search(it, thr):
        bit = jnp.left_shift(jnp.int32(1), 30 - it)
        out = []
        for e in range(n_exp):
            bits = lax.bitcast_convert_type(aff_ref[e], I32)
            cand = thr[e] | bit
            c = total(jnp.where(bits >= cand, 1.0, 0.0))
            out.append(jnp.where(c >= cap, cand, thr[e]))
        return tuple(out)

    thr = lax.fori_loop(0, 31, search, tuple(jnp.zeros((1, 1), I32) for _ in range(n_exp)))
    for e in range(n_exp):
        thr_ref[e] = jnp.broadcast_to(thr[e], (SUBLANES, LANES))

    def tri():
        r_i = lax.broadcasted_iota(I32, (tl, tl), 0)
        c_i = lax.broadcasted_iota(I32, (tl, tl), 1)
        return jnp.where(c_i <= r_i, 1.0, 0.0).astype(BF16)

    def before(col_form=False):
        rb = lax.broadcasted_iota(I32, (nblk, nblk), 0)
        cb = lax.broadcasted_iota(I32, (nblk, nblk), 1)
        return jnp.where((cb < rb) if col_form else (rb < cb), 1.0, 0.0).astype(BF16)

    def prefix(mask):
        incl = jnp.dot(tri(), mask.astype(BF16), preferred_element_type=F32)
        tot8 = jnp.broadcast_to(incl[tl - 1:tl, :], (SUBLANES, nblk)).astype(BF16)
        base = jnp.dot(tot8, before(), preferred_element_type=F32)[0:1, :]
        return incl, base

    for r in range(n_exp):
        wt_ref[r] = jnp.zeros((tl, nblk), F32)

    def select(e, rank):
        aff = aff_ref[e]
        bits = lax.bitcast_convert_type(aff, I32)
        thr_e = thr_ref[e][0:1, 0:1]
        gt = bits > thr_e
        eq = bits == thr_e
        need = cap - total(jnp.where(gt, 1.0, 0.0))
        eqf = jnp.where(eq, 1.0, 0.0)
        incl_eq, base_eq = prefix(eqf)
        eq_rank = base_eq + incl_eq - eqf
        sel = gt | (eq & (eq_rank < need))
        self_ = jnp.where(sel, 1.0, 0.0)
        for r in range(n_exp):
            wt_ref[r] = jnp.where(sel & (rank == r), aff, wt_ref[r])
        sel_ref[e] = self_
        rank_ref[e] = rank
        return rank + self_

    rank = lax.fori_loop(0, n_exp, select, jnp.zeros((tl, nblk), F32))

    incl_n = jnp.dot(tri(), rank.astype(BF16), preferred_element_type=F32)
    tot_n = jnp.broadcast_to(incl_n[tl - 1:tl, :], (SUBLANES, nblk))
    tot_hi = jnp.floor(tot_n * (1.0 / DIGIT))
    tot_lo = tot_n - tot_hi * DIGIT
    base_n = (DIGIT * jnp.dot(tot_hi.astype(BF16), before(), preferred_element_type=F32)
              + jnp.dot(tot_lo.astype(BF16), before(), preferred_element_type=F32))[0:1, :]
    off_ref[...] = base_n + incl_n - rank

    def place(e, carry):
        slot_f = lax.broadcasted_iota(I32, (nblk, cap), 1).astype(F32)
        blk_f = lax.broadcasted_iota(I32, (nblk, cap), 0).astype(F32)
        tl_f = lax.broadcasted_iota(I32, (tl, cap), 0).astype(F32)
        self_ = sel_ref[e]
        incl, _ = prefix(self_)
        tot_c = jnp.sum(self_.T, axis=1, keepdims=True)
        base_c = jnp.dot(before(col_form=True), jnp.broadcast_to(tot_c, (nblk, LANES)).astype(BF16),
                         preferred_element_type=F32)[:, 0:1]
        owns = (base_c <= slot_f) & (slot_f < base_c + tot_c)
        local = jnp.sum(jnp.where(owns, slot_f - base_c, 0.0), axis=0, keepdims=True)
        blk = jnp.sum(jnp.where(owns, blk_f, 0.0), axis=0, keepdims=True)
        owns_b = jnp.where(owns, 1.0, 0.0).astype(BF16)
        incl_at = jnp.dot(incl.astype(BF16), owns_b, preferred_element_type=F32)
        pos = jnp.sum(jnp.where(incl_at <= local, 1.0, 0.0), axis=0, keepdims=True)
        idx_ref[e] = (blk * tl + pos).astype(I32)

        row = off_ref[...] + rank_ref[e]
        d2 = jnp.floor(row * (1.0 / (DIGIT * DIGIT)))
        rem = row - d2 * (DIGIT * DIGIT)
        d1 = jnp.floor(rem * (1.0 / DIGIT))
        d0 = rem - d1 * DIGIT
        at = lambda dg: jnp.dot(dg.astype(BF16), owns_b, preferred_element_type=F32)
        row_at = (DIGIT * DIGIT) * at(d2) + DIGIT * at(d1) + at(d0)
        dst_ref[e] = jnp.sum(jnp.where(tl_f == pos, row_at, 0.0), axis=0, keepdims=True).astype(I32)
        return carry

    lax.fori_loop(0, n_exp, place, 0)


def _route(lg, cap):
    n_exp, tl, nblk = lg.shape
    full = lambda: pl.BlockSpec(memory_space=pltpu.VMEM)
    return pl.pallas_call(
        functools.partial(_route_body, cap=cap),
        in_specs=[full()],
        out_specs=[full() for _ in range(4)],
        out_shape=[
            jax.ShapeDtypeStruct((n_exp, 1, cap), I32),
            jax.ShapeDtypeStruct((n_exp, 1, cap), I32),
            jax.ShapeDtypeStruct((n_exp, tl, nblk), F32),
            jax.ShapeDtypeStruct((tl, nblk), F32),
        ],
        scratch_shapes=[pltpu.VMEM((n_exp, tl, nblk), F32) for _ in range(3)]
                       + [pltpu.VMEM((n_exp, SUBLANES, LANES), I32)],
        compiler_params=pltpu.CompilerParams(vmem_limit_bytes=VMEM_LIMIT),
        name="route",
    )(lg)


def _expert_body(idx_ref, nxt_ref, prev_dst_ref, dst_ref, h_ref, wg_ref, wu_ref, wd_ref, z_ref,
                 rows, x_scr, acc_ref, y_scr, sem, out_sem, *, tc):
    e, c, f = pl.program_id(0), pl.program_id(1), pl.program_id(2)
    tile = e * pl.num_programs(1) + c
    n_tiles = pl.num_programs(0) * pl.num_programs(1)
    last_f = pl.num_programs(2) - 1
    padded, half = rows.shape
    per_step = padded // EXPERT_FF_STEPS

    def gather_row(tok_ref, r):
        return pltpu.make_async_copy(h_ref.at[pl.ds(tok_ref[0, r], 1)], rows.at[pl.ds(r, 1)], sem)

    def scatter_row(row_ref, r):
        return pltpu.make_async_copy(y_scr.at[pl.ds(r, 1)], z_ref.at[pl.ds(row_ref[0, r], 1)], out_sem)

    def all_gathered():
        return pltpu.make_async_copy(h_ref.at[pl.ds(0, padded)], rows, sem)

    def all_scattered(n_rows):
        return pltpu.make_async_copy(y_scr.at[pl.ds(0, n_rows)], z_ref.at[pl.ds(0, n_rows)], out_sem)

    def in_groups(n_rows, start_row):
        def group(g, carry):
            first = pl.multiple_of(g * ROW_UNROLL, ROW_UNROLL)
            for k in range(ROW_UNROLL):
                start_row(first + k)
            return carry
        lax.fori_loop(0, n_rows // ROW_UNROLL, group, 0)

    @pl.when((tile == 0) & (f == 0))
    def _():
        y_scr[...] = jnp.zeros_like(y_scr)
        in_groups(padded, lambda r: gather_row(idx_ref, r).start())

    @pl.when(f == 0)
    def _():
        all_gathered().wait()
        hi, lo = _unpack_bf16_pairs(rows[0:tc, :])
        x_scr[:, :half] = hi.astype(BF16)
        x_scr[:, half:] = lo.astype(BF16)
        acc_ref[...] = jnp.zeros_like(acc_ref)

    for g in range(per_step // ROW_UNROLL):
        first = pl.multiple_of(f * per_step + g * ROW_UNROLL, ROW_UNROLL)
        for k in range(ROW_UNROLL):
            gather_row(nxt_ref, first + k).start(priority=ROW_DMA_QUEUE)
            scatter_row(prev_dst_ref, first + k).start(priority=ROW_DMA_QUEUE)

    x = x_scr[...]
    a = jnp.dot(x, wg_ref[...].astype(BF16), preferred_element_type=F32)
    b = jnp.dot(x, wu_ref[...].astype(BF16), preferred_element_type=F32)
    hmid = (a * jax.nn.sigmoid(a) * b).astype(BF16)
    acc_ref[...] += jnp.dot(hmid, wd_ref[...].astype(BF16), preferred_element_type=F32)

    @pl.when(f == last_f)
    def _():
        all_scattered(padded).wait()
        y_scr[0:tc, :] = _pack_bf16_pairs(acc_ref[...].astype(BF16).astype(F32))

        @pl.when(tile == n_tiles - 1)
        def _():
            in_groups(tc, lambda r: scatter_row(dst_ref, r).start())
            all_scattered(tc).wait()
            all_gathered().wait()


def _experts(idx, dst, h2p, w_gate, w_up, w_down, cap):
    n_exp, d, ff = w_gate.shape
    half = h2p.shape[1]
    tc = idx.shape[-1]
    tf = ff // EXPERT_FF_STEPS
    ct = cap // tc
    n_tiles = n_exp * ct
    assert cap % tc == 0 and ff % EXPERT_FF_STEPS == 0 and tf % LANES == 0 and tc % ROW_UNROLL == 0
    per_step = -(-tc // (EXPERT_FF_STEPS * ROW_UNROLL)) * ROW_UNROLL
    padded = per_step * EXPERT_FF_STEPS
    spare = padded - tc
    idx = jnp.pad(idx, ((0, 0), (0, 0), (0, spare)))
    dst = jnp.concatenate(
        [dst, jnp.broadcast_to(n_exp * cap + jnp.arange(spare, dtype=I32), (n_tiles, 1, spare))], axis=2)
    smem = lambda im: pl.BlockSpec((None, 1, padded), im, memory_space=pltpu.SMEM)
    return pl.pallas_call(
        functools.partial(_expert_body, tc=tc),
        grid=(n_exp, ct, EXPERT_FF_STEPS),
        in_specs=[
            smem(lambda e, c, f: (e * ct + c, 0, 0)),
            smem(lambda e, c, f: (jnp.minimum(e * ct + c + 1, n_tiles - 1), 0, 0)),
            smem(lambda e, c, f: (jnp.maximum(e * ct + c - 1, 0), 0, 0)),
            smem(lambda e, c, f: (e * ct + c, 0, 0)),
            pl.BlockSpec(memory_space=pl.ANY),
            pl.BlockSpec((None, d, tf), lambda e, c, f: (e, 0, f)),
            pl.BlockSpec((None, d, tf), lambda e, c, f: (e, 0, f)),
            pl.BlockSpec((None, tf, d), lambda e, c, f: (e, f, 0)),
        ],
        out_specs=pl.BlockSpec(memory_space=pl.ANY),
        out_shape=jax.ShapeDtypeStruct((n_exp * cap + max(spare, SUBLANES), half), U32),
        scratch_shapes=[pltpu.VMEM((padded, half), U32), pltpu.VMEM((tc, d), BF16), pltpu.VMEM((tc, d), F32),
                        pltpu.VMEM((padded, half), U32), pltpu.SemaphoreType.DMA(()),
                        pltpu.SemaphoreType.DMA(())],
        compiler_params=pltpu.CompilerParams(
            dimension_semantics=("arbitrary", "arbitrary", "arbitrary"), vmem_limit_bytes=VMEM_LIMIT,
            has_side_effects=True),
        name="experts",
    )(idx, idx, dst, dst, h2p, w_gate, w_up, w_down)


def _combine_body(start_ref, x2_ref, off_ref, wt_ref, z_ref, o_ref, buf, sem):
    i = pl.program_id(0)
    nsteps = pl.num_programs(0)
    _, _, rc, half = buf.shape
    n_rows = z_ref.shape[0]

    def first_row(b):
        return (start_ref[b] // SUBLANES) * SUBLANES

    def n_chunks(b):
        return (start_ref[b + 1] - first_row(b) + rc - 1) // rc

    def chunk_copy(b, k, slot):
        row = pl.multiple_of(jnp.minimum(first_row(b) + k * rc, n_rows - rc), SUBLANES)
        return pltpu.make_async_copy(z_ref.at[pl.ds(row, rc)], buf.at[slot, k], sem.at[slot])

    def fetch(b, slot):
        def one(k, carry):
            chunk_copy(b, k, slot).start()
            return carry
        lax.fori_loop(0, n_chunks(b), one, 0)

    n_slots = buf.shape[0]

    @pl.when(i == 0)
    def _():
        fetch(0, 0)
        fetch(1, 1)

    @pl.when(i + 2 < nsteps)
    def _():
        fetch(i + 2, (i + 2) % n_slots)

    slot = i % n_slots
    o_ref[...] = x2_ref[...]
    off = off_ref[...]
    sub = lax.broadcasted_iota(I32, (rc, 1), 0)

    def landed(k, carry):
        chunk_copy(i, k, slot).wait()
        return carry

    lax.fori_loop(0, n_chunks(i), landed, 0)

    def chunk(k, carry):
        want = first_row(i) + k * rc
        row = jnp.minimum(want, n_rows - rc) + sub
        rel = jnp.where(row >= want, row.astype(F32) - off, -1.0)
        a = jnp.zeros(rel.shape, F32)
        for r in range(wt_ref.shape[0]):
            a = jnp.where(rel == r, wt_ref[r:r + 1, :], a)
        a2 = a.astype(BF16)
        z_hi, z_lo = _unpack_bf16_pairs(buf[slot, k])

        def seg(zz):
            return lax.dot_general(a2, zz.astype(BF16), (((0,), (0,)), ((), ())), preferred_element_type=F32)
        o_ref[:, :half] += seg(z_hi)
        o_ref[:, half:] += seg(z_lo)
        return carry

    lax.fori_loop(0, n_chunks(i), chunk, 0)


def _combine(starts, x2, off, wt, z):
    n, d = x2.shape
    tb = COMBINE_TOKENS
    rc = COMBINE_ROWS
    max_chunks = (N_EXPERTS * tb + SUBLANES - 1 + rc - 1) // rc + 1
    assert n % tb == 0 and n // tb >= 2 and z.shape[0] % SUBLANES == 0 and z.shape[0] >= rc
    return pl.pallas_call(
        _combine_body,
        grid_spec=pltpu.PrefetchScalarGridSpec(
            num_scalar_prefetch=1,
            grid=(n // tb,),
            in_specs=[
                pl.BlockSpec((tb, d), lambda i, s: (i, 0)),
                pl.BlockSpec((1, tb), lambda i, s: (0, i)),
                pl.BlockSpec((N_EXPERTS, tb), lambda i, s: (0, i)),
                pl.BlockSpec(memory_space=pl.ANY),
            ],
            out_specs=pl.BlockSpec((tb, d), lambda i, s: (i, 0)),
            scratch_shapes=[pltpu.VMEM((COMBINE_SLOTS, max_chunks, rc, d // 2), U32),
                            pltpu.SemaphoreType.DMA((COMBINE_SLOTS,))],
        ),
        out_shape=jax.ShapeDtypeStruct((n, d), F32),
        compiler_params=pltpu.CompilerParams(
            dimension_semantics=("arbitrary",), vmem_limit_bytes=VMEM_LIMIT),
        name="combine",
    )(starts, x2, off, wt, z)


def _layer(x, p):
    b, t, d = x.shape
    n = b * t
    head_dim = p["q_norm"].shape[-1]
    attn = N_HEADS * head_dim
    q, k, v, u = _in_proj(x.reshape(n, d), p["norm_mix"], p["w_in"], p["q_norm"], p["k_norm"],
                          attn=attn, head_dim=head_dim)
    shp = lambda a: a.reshape(b, t, a.shape[-1])
    x2, h2p, lg = _mix(shp(q), shp(k), shp(v), shp(u), x, p["bias_tabs"], p["w_pool"], p["pool_scale"],
                       p["w_out"], p["norm_ffn"], p["wr"], p["wr0"], head_dim=head_dim)

    nblk = n // LANES
    cap = max(1, (CAPACITY_FACTOR * n) // N_EXPERTS)
    lg = lg.reshape(n, LANES)[:, :N_EXPERTS]
    lg = lg.T.reshape(N_EXPERTS, nblk, LANES).transpose(0, 2, 1)
    idx, dst, wt, off = _route(lg, cap)

    tc = min(EXPERT_TOKENS, cap)
    tiles = lambda a: a.reshape(N_EXPERTS * (cap // tc), 1, tc)
    z = _experts(tiles(idx), tiles(dst), h2p.reshape(n, d // 2), p["w_gate"], p["w_up"], p["w_down"], cap)

    assert COMBINE_TOKENS == LANES
    starts = jnp.concatenate([off[0, :].astype(I32), jnp.full((1,), N_EXPERTS * cap, I32)])
    wt = wt.transpose(0, 2, 1).reshape(N_EXPERTS, n)
    out = _combine(starts, x2.reshape(n, d), off.T.reshape(1, n), wt, z)
    return out.reshape(b, t, d)


def _prepare(l, norm_mix, w_in, q_norm, k_norm, rel_pos_bias, w_pool, pool_scale, w_out, norm_ffn,
             w_router, w_gate, w_up, w_down):
    d = w_in.shape[1]
    whi = w_router[l].astype(BF16)
    wlo = (w_router[l] - whi.astype(F32)).astype(BF16)
    pad = lambda *cols: jnp.concatenate(
        list(cols) + [jnp.zeros((d, LANES - sum(c.shape[1] for c in cols)), BF16)], axis=1)
    return dict(
        norm_mix=norm_mix[l][None, :], w_in=w_in[l].astype(BF16),
        q_norm=q_norm[l][None, :], k_norm=k_norm[l][None, :],
        bias_tabs=_bias_tables(rel_pos_bias[l]), w_pool=w_pool[l].astype(BF16),
        pool_scale=pool_scale[l][None, :], w_out=w_out[l].astype(BF16), norm_ffn=norm_ffn[l][None, :],
        wr=pad(whi, wlo), wr0=pad(whi),
        w_gate=w_gate[l], w_up=w_up[l], w_down=w_down[l],
    )


def kernel(x_prompt, x_sample, norm_mix, w_in, q_norm, k_norm, rel_pos_bias, w_pool, pool_scale, w_out,
           norm_ffn, w_router, w_gate, w_up, w_down):
    weights = (norm_mix, w_in, q_norm, k_norm, rel_pos_bias, w_pool, pool_scale, w_out, norm_ffn,
               w_router, w_gate, w_up, w_down)
    layers = [_prepare(l, *weights) for l in range(w_in.shape[0])]

    def trunk(x):
        for p in layers:
            x = _layer(x, p)
        return x

    return trunk(x_prompt), trunk(x_sample)
```
